```python
import math
import jax, jax.numpy as jnp
from jax import lax
import numpy as np

D_MODEL = 1024
BATCH = 32
SEQ = 2048
DEPTH = 4

CHUNK = 64
N_MIXERS = 3
PLE_DIM = 256
D_FF = 4 * D_MODEL
NORM_EPS = 1e-6
SSM_GROUP = 16
SSM_GROUPS = D_MODEL // SSM_GROUP
SSM_STATE = 64
DT_MIN = 1e-3
DT_MAX = 1e-1
CONV_WIDTH = 3
FOX_HEAD_DIM = 64
FOX_HEADS = D_MODEL // FOX_HEAD_DIM
Q_BLOCK = 128

N_SSM_LAYERS = len(range(0, DEPTH, N_MIXERS))
N_CONV_LAYERS = len(range(1, DEPTH, N_MIXERS))
N_FOX_LAYERS = len(range(2, DEPTH, N_MIXERS))

kernel_name = "interleaved_s5_shortconv_fox_trunk"


def rmsnorm(x, gain):
    xf = x.astype(jnp.float32)
    out = xf * lax.rsqrt(jnp.mean(xf * xf, axis=-1, keepdims=True) + NORM_EPS) * gain.astype(jnp.float32)
    return out.astype(x.dtype)


def s5_mixer(u, lam_re, lam_im, log_dt, b_re, b_im, c_re, c_im, d_skip, w_glu):
    f32 = jnp.float32
    bsz, seqlen, _ = u.shape
    uf = u.astype(f32)
    ug = uf.reshape(bsz, seqlen, SSM_GROUPS, SSM_GROUP)
    lr = lam_re.astype(f32)
    li = lam_im.astype(f32)
    dt = jnp.exp(log_dt.astype(f32))[:, None]
    mag = jnp.exp(lr * dt)
    ab_re = mag * jnp.cos(li * dt)
    ab_im = mag * jnp.sin(li * dt)
    nr = ab_re - 1.0
    ni = ab_im
    den = lr * lr + li * li
    coef_re = (nr * lr + ni * li) / den
    coef_im = (ni * lr - nr * li) / den
    br = b_re.astype(f32)
    bi = b_im.astype(f32)
    bb_re = coef_re[..., None] * br - coef_im[..., None] * bi
    bb_im = coef_re[..., None] * bi + coef_im[..., None] * br
    bu_re = jnp.einsum('blgh,gph->blgp', ug, bb_re)
    bu_im = jnp.einsum('blgh,gph->blgp', ug, bb_im)
    a_re = jnp.broadcast_to(ab_re, (1, seqlen, SSM_GROUPS, SSM_STATE))
    a_im = jnp.broadcast_to(ab_im, (1, seqlen, SSM_GROUPS, SSM_STATE))

    def combine(left, right):
        a1r, a1i, b1r, b1i = left
        a2r, a2i, b2r, b2i = right
        return (a2r * a1r - a2i * a1i,
                a2r * a1i + a2i * a1r,
                a2r * b1r - a2i * b1i + b2r,
                a2r * b1i + a2i * b1r + b2i)

    _, _, xs_re, xs_im = lax.associative_scan(combine, (a_re, a_im, bu_re, bu_im), axis=1)
    y = (jnp.einsum('blgp,ghp->blgh', xs_re, c_re.astype(f32))
         - jnp.einsum('blgp,ghp->blgh', xs_im, c_im.astype(f32)))
    y = y.reshape(bsz, seqlen, D_MODEL) + d_skip.astype(f32) * uf
    y = jax.nn.gelu(y).astype(u.dtype)
    a, g = jnp.split(y @ w_glu, 2, axis=-1)
    return a * jax.nn.sigmoid(g)


def short_conv_mixer(h, w_in, conv_w, w_out):
    b_gate, c_gate, v = jnp.split(h @ w_in, 3, axis=-1)
    z = c_gate * v
    rhs = conv_w[:, None, :].astype(z.dtype)
    conv = lax.conv_general_dilated(
        z, rhs, window_strides=(1,), padding=[(CONV_WIDTH - 1, 0)],
        dimension_numbers=('NWC', 'WIO', 'NWC'), feature_group_count=D_MODEL)
    return (b_gate * conv) @ w_out


def fox_mixer(h, w_in, b_f, w_out):
    f32 = jnp.float32
    bsz, seqlen, _ = h.shape
    proj = h @ w_in
    q, k, v, f_logit = jnp.split(proj, [D_MODEL, 2 * D_MODEL, 3 * D_MODEL], axis=-1)
    q = q.reshape(bsz, seqlen, FOX_HEADS, FOX_HEAD_DIM)
    k = k.reshape(bsz, seqlen, FOX_HEADS, FOX_HEAD_DIM)
    v = v.reshape(bsz, seqlen, FOX_HEADS, FOX_HEAD_DIM)
    log_f = jax.nn.log_sigmoid(f_logit.astype(f32) + b_f.astype(f32))
    cum = jnp.cumsum(log_f, axis=1).transpose(0, 2, 1)
    scale = FOX_HEAD_DIM ** -0.5
    outs = []
    for blk in range(seqlen // Q_BLOCK):
        q0 = blk * Q_BLOCK
        kv_len = q0 + Q_BLOCK
        s = jnp.einsum('bqhd,bkhd->bhqk', q[:, q0:kv_len], k[:, :kv_len]).astype(f32) * scale
        decay = cum[:, :, q0:kv_len, None] - cum[:, :, None, :kv_len]
        causal = (q0 + jnp.arange(Q_BLOCK))[:, None] >= jnp.arange(kv_len)[None, :]
        s = jnp.where(causal, s + decay, -jnp.inf)
        pr = jax.nn.softmax(s, axis=-1).astype(v.dtype)
        outs.append(jnp.einsum('bhqk,bkhd->bqhd', pr, v[:, :kv_len]))
    o = jnp.concatenate(outs, axis=1).reshape(bsz, seqlen, D_MODEL)
    return o @ w_out


def sqrelu_mlp(h, w1, w2):
    a = jax.nn.relu(h @ w1)
    return (a * a) @ w2


def setup_inputs(seed: int = 0) -> dict:
    key = jax.random.key(seed)
    ks = jax.random.split(key, 26)
    f32 = jnp.float32
    nrm = lambda k, shape, s: jax.random.normal(k, shape, f32) * s
    gain = lambda k, shape: 1.0 + 0.05 * jax.random.normal(k, shape, f32)
    ns, nc, nf = N_SSM_LAYERS, N_CONV_LAYERS, N_FOX_LAYERS
    G, P, H = SSM_GROUPS, SSM_STATE, SSM_GROUP
    lam_im_init = jnp.broadcast_to(jnp.arange(P, dtype=f32) * math.pi, (ns, G, P))
    return {
        "x": nrm(ks[0], (BATCH, SEQ, D_MODEL), 1.0),
        "p": nrm(ks[1], (DEPTH, BATCH, SEQ, PLE_DIM), 1.0),
        "norm_mix": gain(ks[2], (DEPTH, D_MODEL)),
        "norm_ffn": gain(ks[3], (DEPTH, D_MODEL)),
        "norm_ple": gain(ks[4], (DEPTH, D_MODEL)),
        "norm_final": gain(ks[5], (D_MODEL,)),
        "ssm_lam_re": -0.5 + 0.01 * jax.random.normal(ks[6], (ns, G, P), f32),
        "ssm_lam_im": lam_im_init + 0.01 * jax.random.normal(ks[7], (ns, G, P), f32),
        "ssm_log_dt": jax.random.uniform(ks[8], (ns, G), f32, math.log(DT_MIN), math.log(DT_MAX)),
        "ssm_b_re": nrm(ks[9], (ns, G, P, H), (2 * H) ** -0.5),
        "ssm_b_im": nrm(ks[10], (ns, G, P, H), (2 * H) ** -0.5),
        "ssm_c_re": nrm(ks[11], (ns, G, H, P), (2 * P) ** -0.5),
        "ssm_c_im": nrm(ks[12], (ns, G, H, P), (2 * P) ** -0.5),
        "ssm_d": nrm(ks[13], (ns, D_MODEL), 1.0),
        "ssm_w_glu": nrm(ks[14], (ns, D_MODEL, 2 * D_MODEL), D_MODEL ** -0.5),
        "conv_w_in": nrm(ks[15], (nc, D_MODEL, 3 * D_MODEL), D_MODEL ** -0.5),
        "conv_w": nrm(ks[16], (nc, CONV_WIDTH, D_MODEL), CONV_WIDTH ** -0.5),
        "conv_w_out": nrm(ks[17], (nc, D_MODEL, D_MODEL), D_MODEL ** -0.5),
        "fox_w_in": nrm(ks[18], (nf, D_MODEL, 3 * D_MODEL + FOX_HEADS), D_MODEL ** -0.5),
        "fox_b_f": 3.0 + 0.5 * jax.random.normal(ks[19], (nf, FOX_HEADS), f32),
        "fox_w_out": nrm(ks[20], (nf, D_MODEL, D_MODEL), D_MODEL ** -0.5),
        "mlp_w1": nrm(ks[21], (DEPTH, D_MODEL, D_FF), D_MODEL ** -0.5),
        "mlp_w2": nrm(ks[22], (DEPTH, D_FF, D_MODEL), D_FF ** -0.5),
        "ple_w": nrm(ks[23], (DEPTH, PLE_DIM, D_MODEL), PLE_DIM ** -0.5),
        "ple_gate_w": nrm(ks[24], (DEPTH, D_MODEL, D_MODEL), D_MODEL ** -0.5),
    }


def reference(x, p, norm_mix, norm_ffn, norm_ple, norm_final,
              ssm_lam_re, ssm_lam_im, ssm_log_dt, ssm_b_re, ssm_b_im, ssm_c_re, ssm_c_im,
              ssm_d, ssm_w_glu, conv_w_in, conv_w, conv_w_out, fox_w_in, fox_b_f, fox_w_out,
              mlp_w1, mlp_w2, ple_w, ple_gate_w):
    for i in range(DEPTH):
        kind, slot = i % N_MIXERS, i // N_MIXERS
        h = rmsnorm(x, norm_mix[i])
        if kind == 0:
            mix = s5_mixer(h, ssm_lam_re[slot], ssm_lam_im[slot], ssm_log_dt[slot],
                           ssm_b_re[slot], ssm_b_im[slot], ssm_c_re[slot], ssm_c_im[slot],
                           ssm_d[slot], ssm_w_glu[slot])
        elif kind == 1:
            mix = short_conv_mixer(h, conv_w_in[slot], conv_w[slot], conv_w_out[slot])
        else:
            mix = fox_mixer(h, fox_w_in[slot], fox_b_f[slot], fox_w_out[slot])
        x = x + mix
        x = x + sqrelu_mlp(rmsnorm(x, norm_ffn[i]), mlp_w1[i], mlp_w2[i])
        gate = jax.nn.sigmoid(rmsnorm(x, norm_ple[i]) @ ple_gate_w[i])
        x = x + (p[i] @ ple_w[i]) * gate
    return rmsnorm(x, norm_final)
```

```python
import functools
import math

import jax
import jax.numpy as jnp
from jax import lax
from jax.experimental import pallas as pl
from jax.experimental.pallas import tpu as pltpu

D_MODEL = 1024
D_FF = 4 * D_MODEL
PLE_DIM = 256
NORM_EPS = 1e-6
N_MIXERS = 3
SSM_GROUP = 16
SSM_GROUPS = D_MODEL // SSM_GROUP
SSM_STATE = 64
SSM_STEP = 16
SSM_PAIR_IN = 2 * SSM_STEP * SSM_GROUP
CONV_WIDTH = 3
FOX_HEAD_DIM = 64
FOX_HEADS = D_MODEL // FOX_HEAD_DIM
LANES = 128
SUBLANES = 8
VMEM_LIMIT = 56 * 1024 * 1024

ROW_BLOCK = 512
FF_CHUNK = 512
ATTN_Q_BLOCK = 256

BF16 = jnp.bfloat16
F32 = jnp.float32


def _rms(x, gain):
    return x * lax.rsqrt(jnp.mean(x * x, axis=-1, keepdims=True) + NORM_EPS) * gain


def _dot(a, b):
    return jnp.dot(a, b, preferred_element_type=F32)


def _params(n_axes):
    return pltpu.CompilerParams(dimension_semantics=("arbitrary",) * n_axes,
                                vmem_limit_bytes=VMEM_LIMIT)


def _resident(shape):
    nd = len(shape)
    return pl.BlockSpec(shape, lambda *_: (0,) * nd, pipeline_mode=pl.Buffered(1))


def _ffn_ple_kernel(x_ref, p_ref, gf_ref, gp_ref, gl_ref, w1_ref, w2_ref, wg_ref, wp_ref,
                    o_ref, acc_ref, *, final_norm):
    x = x_ref[...]
    h = _rms(x, gf_ref[...]).astype(BF16)
    acc_ref[...] = jnp.zeros_like(acc_ref)

    def chunk(c, carry):
        col = pl.multiple_of(c * FF_CHUNK, FF_CHUNK)
        a = jnp.maximum(_dot(h, w1_ref[:, pl.ds(col, FF_CHUNK)]), 0.0)
        acc_ref[...] += _dot((a * a).astype(BF16), w2_ref[pl.ds(col, FF_CHUNK), :])
        return carry

    lax.fori_loop(0, D_FF // FF_CHUNK, chunk, 0)
    x = x + acc_ref[...]
    gate = jax.nn.sigmoid(_dot(_rms(x, gp_ref[...]).astype(BF16), wg_ref[...]))
    x = x + _dot(p_ref[...].astype(BF16), wp_ref[...]) * gate
    if final_norm:
        x = _rms(x, gl_ref[...])
    o_ref[...] = x


def _ffn_ple(x, p, g_ffn, g_ple, g_last, w1, w2, wg, wp, final_norm):
    n = x.shape[0]
    row = lambda w: pl.BlockSpec((ROW_BLOCK, w), lambda i: (i, 0))
    return pl.pallas_call(
        functools.partial(_ffn_ple_kernel, final_norm=final_norm),
        grid=(n // ROW_BLOCK,),
        in_specs=[row(D_MODEL), row(PLE_DIM),
                  _resident((1, D_MODEL)), _resident((1, D_MODEL)), _resident((1, D_MODEL)),
                  _resident((D_MODEL, D_FF)), _resident((D_FF, D_MODEL)),
                  _resident((D_MODEL, D_MODEL)), _resident((PLE_DIM, D_MODEL))],
        out_specs=row(D_MODEL),
        out_shape=jax.ShapeDtypeStruct((n, D_MODEL), F32),
        scratch_shapes=[pltpu.VMEM((ROW_BLOCK, D_MODEL), F32)],
        compiler_params=_params(1),
        name="ffn_ple",
    )(x, p, g_ffn, g_ple, g_last, w1, w2, wg, wp)


def _rms_cast_kernel(x_ref, g_ref, o_ref):
    o_ref[...] = _rms(x_ref[...], g_ref[...]).astype(BF16)


def _rms_cast(x, gain):
    n = x.shape[0]
    row = pl.BlockSpec((ROW_BLOCK, D_MODEL), lambda i: (i, 0))
    return pl.pallas_call(
        _rms_cast_kernel, grid=(n // ROW_BLOCK,),
        in_specs=[row, _resident((1, D_MODEL))], out_specs=row,
        out_shape=jax.ShapeDtypeStruct((n, D_MODEL), BF16),
        compiler_params=_params(1), name="rms_cast",
    )(x, gain)


def _s5_core_kernel(u_ref, bs_ref, cs_ref, ds_ref, a_ref, y_ref, bu_ref, xs_ref, *, batch):
    half = SSM_PAIR_IN // 2
    state = 2 * SSM_STATE
    u = u_ref[0]
    bu_ref[...] = _dot(u, bs_ref[0])
    a_re = jnp.broadcast_to(a_ref[0, 0:1, :], (batch, state))
    a_im = jnp.broadcast_to(a_ref[0, 1:2, :], (batch, state))
    steps = u.shape[0] // batch

    def step(k, carry):
        s_re, s_im = carry
        r0 = pl.multiple_of(k * batch, batch)
        xs_ref[pl.ds(r0, batch), 0:state] = s_re
        xs_ref[pl.ds(r0, batch), state:2 * state] = s_im
        b_re = bu_ref[pl.ds(r0, batch), 0:state]
        b_im = bu_ref[pl.ds(r0, batch), state:2 * state]
        return (a_re * s_re - a_im * s_im + b_re, a_re * s_im + a_im * s_re + b_im)

    zero = jnp.zeros((batch, state), F32)
    lax.fori_loop(0, steps, step, (zero, zero))
    y_ref[0, :, 0:half] = (_dot(xs_ref[...].astype(BF16), cs_ref[0, :, 0:half])
                           + _dot(u[:, 0:half], ds_ref[0, 0]))
    y_ref[0, :, half:] = (_dot(xs_ref[...].astype(BF16), cs_ref[0, :, half:])
                          + _dot(u[:, half:], ds_ref[0, 1]))


def _s5_core(u_t, bs, cs, ds, a16, batch):
    pairs, rows, _ = u_t.shape
    blk = lambda *s: pl.BlockSpec((1,) + s, lambda q: (q,) + (0,) * len(s))
    return pl.pallas_call(
        functools.partial(_s5_core_kernel, batch=batch), grid=(pairs,),
        in_specs=[blk(rows, SSM_PAIR_IN), blk(SSM_PAIR_IN, 4 * SSM_STATE),
                  blk(4 * SSM_STATE, SSM_PAIR_IN),
                  blk(2, SSM_PAIR_IN // 2, SSM_PAIR_IN // 2), blk(2, 2 * SSM_STATE)],
        out_specs=blk(rows, SSM_PAIR_IN),
        out_shape=jax.ShapeDtypeStruct((pairs, rows, SSM_PAIR_IN), F32),
        scratch_shapes=[pltpu.VMEM((rows, 4 * SSM_STATE), F32),
                        pltpu.VMEM((rows, 4 * SSM_STATE), F32)],
        compiler_params=_params(1), name="s5_core",
    )(u_t, bs, cs, ds, a16)


def _s5_glu_kernel(x_ref, y_ref, g_ref, d_ref, wa_ref, wg_ref, o_ref):
    x = x_ref[...]
    u = _rms(x, g_ref[...])
    z = jax.nn.gelu(y_ref[...] + d_ref[...] * u).astype(BF16)
    o_ref[...] = x + _dot(z, wa_ref[...]) * jax.nn.sigmoid(_dot(z, wg_ref[...]))


def _s5_glu(x, y, gain, d_skip, wa, wg):
    n = x.shape[0]
    row = pl.BlockSpec((ROW_BLOCK, D_MODEL), lambda i: (i, 0))
    return pl.pallas_call(
        _s5_glu_kernel, grid=(n // ROW_BLOCK,),
        in_specs=[row, row, _resident((1, D_MODEL)), _resident((1, D_MODEL)),
                  _resident((D_MODEL, D_MODEL)), _resident((D_MODEL, D_MODEL))],
        out_specs=row, out_shape=jax.ShapeDtypeStruct((n, D_MODEL), F32),
        compiler_params=_params(1), name="s5_glu",
    )(x, y, gain, d_skip, wa, wg)


def _s5_weights(lam_re, lam_im, log_dt, b_re, b_im, c_re, c_im):
    hp = lax.Precision.HIGHEST
    dt = jnp.exp(log_dt)[:, None]
    n = jnp.arange(SSM_STEP + 1, dtype=F32)[:, None, None]
    mag = jnp.exp(n * (lam_re * dt))
    pw_re = mag * jnp.cos(n * (lam_im * dt))
    pw_im = mag * jnp.sin(n * (lam_im * dt))
    nr, ni = pw_re[1] - 1.0, pw_im[1]
    den = lam_re * lam_re + lam_im * lam_im
    coef_re = (nr * lam_re + ni * lam_im) / den
    coef_im = (ni * lam_re - nr * lam_im) / den
    bb_re = coef_re[..., None] * b_re - coef_im[..., None] * b_im
    bb_im = coef_re[..., None] * b_im + coef_im[..., None] * b_re
    ca_re = c_re[None] * pw_re[:, :, None, :] - c_im[None] * pw_im[:, :, None, :]
    ca_im = c_re[None] * pw_im[:, :, None, :] + c_im[None] * pw_re[:, :, None, :]
    rev_re, rev_im = pw_re[SSM_STEP - 1::-1], pw_im[SSM_STEP - 1::-1]
    bs_re = (rev_re[:, :, :, None] * bb_re[None] - rev_im[:, :, :, None] * bb_im[None])
    bs_im = (rev_re[:, :, :, None] * bb_im[None] + rev_im[:, :, :, None] * bb_re[None])
    fold_in = lambda t: t.transpose(1, 0, 3, 2).reshape(SSM_GROUPS, SSM_STEP * SSM_GROUP,
                                                        SSM_STATE)
    bs_re, bs_im = fold_in(bs_re), fold_in(bs_im)
    fold_out = lambda t: t.transpose(1, 3, 0, 2).reshape(SSM_GROUPS, SSM_STATE,
                                                         SSM_STEP * SSM_GROUP)
    cs_re, cs_im = fold_out(ca_re[1:]), fold_out(-ca_im[1:])
    kern = (jnp.einsum('tghp,gpk->tghk', ca_re[:SSM_STEP], bb_re, precision=hp)
            - jnp.einsum('tghp,gpk->tghk', ca_im[:SSM_STEP], bb_im, precision=hp))
    lag = jnp.arange(SSM_STEP)[None, :] - jnp.arange(SSM_STEP)[:, None]
    toep = jnp.where((lag >= 0)[:, :, None, None, None], kern[jnp.maximum(lag, 0)], 0.0)
    ds = toep.transpose(2, 0, 4, 1, 3).reshape(SSM_GROUPS, SSM_STEP * SSM_GROUP,
                                               SSM_STEP * SSM_GROUP)
    pairs = SSM_GROUPS // 2
    z_b = jnp.zeros((pairs, SSM_STEP * SSM_GROUP, SSM_STATE), F32)
    pair_cols = lambda t: jnp.concatenate(
        [jnp.concatenate([t[0::2], z_b], axis=2), jnp.concatenate([z_b, t[1::2]], axis=2)],
        axis=1)
    bs = jnp.concatenate([pair_cols(bs_re), pair_cols(bs_im)], axis=2)
    z_c = jnp.zeros((pairs, SSM_STATE, SSM_STEP * SSM_GROUP), F32)
    pair_rows = lambda t: jnp.concatenate(
        [jnp.concatenate([t[0::2], z_c], axis=2), jnp.concatenate([z_c, t[1::2]], axis=2)],
        axis=1)
    cs = jnp.concatenate([pair_rows(cs_re), pair_rows(cs_im)], axis=1)
    ds = ds.reshape(pairs, 2, SSM_STEP * SSM_GROUP, SSM_STEP * SSM_GROUP)
    a16 = jnp.stack([pw_re[SSM_STEP].reshape(pairs, 2 * SSM_STATE),
                     pw_im[SSM_STEP].reshape(pairs, 2 * SSM_STATE)], axis=1)
    return bs.astype(BF16), cs.astype(BF16), ds.astype(BF16), a16


def _s5_mixer(x, bsz, seqlen, gain, d_skip, weights, wa, wg):
    bs, cs, ds, a16 = weights
    steps, pairs = seqlen // SSM_STEP, SSM_GROUPS // 2
    h = _rms_cast(x, gain)
    u_t = (h.reshape(bsz, steps, SSM_STEP, pairs, 2, SSM_GROUP)
           .transpose(3, 1, 0, 4, 2, 5).reshape(pairs, steps * bsz, SSM_PAIR_IN))
    y_t = _s5_core(u_t, bs, cs, ds, a16, bsz)
    y = (y_t.reshape(pairs, steps, bsz, 2, SSM_STEP, SSM_GROUP)
         .transpose(2, 1, 4, 0, 3, 5).reshape(bsz * seqlen, D_MODEL))
    return _s5_glu(x, y, gain, d_skip, wa, wg)


def _conv_kernel(x_ref, g_ref, wb_ref, wc_ref, wv_ref, cw_ref, wo_ref, o_ref, z_ref):
    t = pl.program_id(1)
    rows = x_ref.shape[0]

    @pl.when(t == 0)
    def _():
        z_ref[0:SUBLANES, :] = jnp.zeros((SUBLANES, D_MODEL), F32)

    x = x_ref[...]
    h = _rms(x, g_ref[...]).astype(BF16)
    z_ref[SUBLANES:, :] = _dot(h, wc_ref[...]) * _dot(h, wv_ref[...])
    conv = cw_ref[CONV_WIDTH - 1:CONV_WIDTH, :] * z_ref[SUBLANES:, :]
    for tap in range(CONV_WIDTH - 1):
        back = CONV_WIDTH - 1 - tap
        conv = conv + cw_ref[tap:tap + 1, :] * z_ref[SUBLANES - back:SUBLANES - back + rows, :]
    z_ref[0:SUBLANES, :] = z_ref[rows:rows + SUBLANES, :]
    gated = (_dot(h, wb_ref[...]) * conv).astype(BF16)
    o_ref[...] = x + _dot(gated, wo_ref[...])


def _conv_mixer(x, bsz, seqlen, gain, wb, wc, wv, conv_w, wo):
    tblocks = seqlen // ROW_BLOCK
    row = pl.BlockSpec((ROW_BLOCK, D_MODEL), lambda b, t: (b * tblocks + t, 0))
    sq = _resident((D_MODEL, D_MODEL))
    return pl.pallas_call(
        _conv_kernel, grid=(bsz, tblocks),
        in_specs=[row, _resident((1, D_MODEL)), sq, sq, sq,
                  _resident((CONV_WIDTH, D_MODEL)), sq],
        out_specs=row, out_shape=jax.ShapeDtypeStruct(x.shape, F32),
        scratch_shapes=[pltpu.VMEM((ROW_BLOCK + SUBLANES, D_MODEL), F32)],
        compiler_params=_params(2), name="conv_mixer",
    )(x, gain, wb, wc, wv, conv_w, wo)


def _fox_proj_kernel(x_ref, g_ref, wq_ref, wk_ref, wv_ref, wf_ref, bf_ref,
                     q_ref, k_ref, v_ref, cum_ref, carry_ref):
    t = pl.program_id(1)
    rows = x_ref.shape[0]

    @pl.when(t == 0)
    def _():
        carry_ref[...] = jnp.zeros_like(carry_ref)

    h = _rms(x_ref[...], g_ref[...]).astype(BF16)
    q_ref[...] = _dot(h, wq_ref[...]).astype(BF16)
    k_ref[...] = _dot(h, wk_ref[...]).astype(BF16)
    v_ref[...] = _dot(h, wv_ref[...]).astype(BF16)
    logit = _dot(h, wf_ref[...]) + bf_ref[...]
    log_f = jnp.minimum(logit, 0.0) - jnp.log1p(jnp.exp(-jnp.abs(logit)))
    tri = (lax.broadcasted_iota(jnp.int32, (rows, rows), 0)
           >= lax.broadcasted_iota(jnp.int32, (rows, rows), 1)).astype(BF16)
    hi = log_f.astype(BF16)
    rest = log_f - hi.astype(F32)
    mid = rest.astype(BF16)
    lo = (rest - mid.astype(F32)).astype(BF16)
    cum = _dot(tri, hi) + _dot(tri, mid) + _dot(tri, lo) + carry_ref[0:1, :]
    cum_ref[...] = cum
    carry_ref[0:1, :] = cum[rows - 1:rows, :]


def _fox_proj(x, bsz, seqlen, gain, wq, wk, wv, wf, bf):
    tblocks = seqlen // ROW_BLOCK
    idx = lambda b, t: (b * tblocks + t, 0)
    row = pl.BlockSpec((ROW_BLOCK, D_MODEL), idx)
    sq = _resident((D_MODEL, D_MODEL))
    qkv = jax.ShapeDtypeStruct(x.shape, BF16)
    return pl.pallas_call(
        _fox_proj_kernel, grid=(bsz, tblocks),
        in_specs=[row, _resident((1, D_MODEL)), sq, sq, sq,
                  _resident((D_MODEL, LANES)), _resident((1, LANES))],
        out_specs=[row, row, row, pl.BlockSpec((ROW_BLOCK, LANES), idx)],
        out_shape=[qkv, qkv, qkv, jax.ShapeDtypeStruct((x.shape[0], LANES), F32)],
        scratch_shapes=[pltpu.VMEM((SUBLANES, LANES), F32)],
        compiler_params=_params(2), name="fox_proj",
    )(x, gain, wq, wk, wv, wf, bf)


def _fox_attn_kernel(q_ref, k_ref, v_ref, cq_ref, ck_ref, o_ref):
    pair = pl.program_id(1)
    seqlen = q_ref.shape[1]
    tq = ATTN_Q_BLOCK
    lane = lax.broadcasted_iota(jnp.int32, (1, LANES), 1)
    low = lane < FOX_HEAD_DIM
    scale = FOX_HEAD_DIM ** -0.5
    causal = (lax.broadcasted_iota(jnp.int32, (tq, tq), 0)
              >= lax.broadcasted_iota(jnp.int32, (tq, tq), 1))
    contract_last = (((1,), (1,)), ((), ()))
    for qb in range(seqlen // tq):
        q0 = qb * tq
        q = q_ref[0, q0:q0 + tq, :] * jnp.asarray(scale, BF16)
        cq_blk = cq_ref[0, q0:q0 + tq, :]
        outs = []
        for e in range(2):
            head = 2 * pair + e
            mine = low if e == 0 else jnp.logical_not(low)
            qm = jnp.where(mine, q, jnp.zeros_like(q))
            cq = jnp.sum(jnp.where(lane == head, cq_blk, 0.0), axis=1, keepdims=True)
            ck = ck_ref[0, pl.ds(head, 1), 0:q0 + tq]
            s_d = lax.dot_general(qm, k_ref[0, q0:q0 + tq, :], contract_last,
                                  preferred_element_type=F32)
            s_d = jnp.where(causal, s_d + (cq - ck[:, q0:q0 + tq]), -jnp.inf)
            m = jnp.max(s_d, axis=1, keepdims=True)
            if qb > 0:
                s_p = lax.dot_general(qm, k_ref[0, 0:q0, :], contract_last,
                                      preferred_element_type=F32)
                s_p = s_p + (cq - ck[:, 0:q0])
                m = jnp.maximum(m, jnp.max(s_p, axis=1, keepdims=True))
            p_d = jnp.exp(s_d - m)
            denom = jnp.sum(p_d, axis=1, keepdims=True)
            acc = _dot(p_d.astype(BF16), v_ref[0, q0:q0 + tq, :])
            if qb > 0:
                p_p = jnp.exp(s_p - m)
                denom = denom + jnp.sum(p_p, axis=1, keepdims=True)
                acc = acc + _dot(p_p.astype(BF16), v_ref[0, 0:q0, :])
            outs.append(acc / denom)
        o_ref[0, q0:q0 + tq, :] = jnp.where(low, outs[0], outs[1]).astype(BF16)


def _fox_attn(q, k, v, cum, cum_t):
    bsz, seqlen, _ = q.shape
    seq = pl.BlockSpec((1, seqlen, LANES), lambda b, j: (b, 0, j))
    return pl.pallas_call(
        _fox_attn_kernel, grid=(bsz, FOX_HEADS // 2),
        in_specs=[seq, seq, seq,
                  pl.BlockSpec((1, seqlen, LANES), lambda b, j: (b, 0, 0)),
                  pl.BlockSpec((1, FOX_HEADS, seqlen), lambda b, j: (b, 0, 0))],
        out_specs=seq, out_shape=jax.ShapeDtypeStruct(q.shape, BF16),
        compiler_params=_params(2), name="fox_attn",
    )(q, k, v, cum, cum_t)


def _fox_out_kernel(x_ref, o_ref, wo_ref, y_ref):
    y_ref[...] = x_ref[...] + _dot(o_ref[...], wo_ref[...])


def _fox_out(x, o, wo):
    n = x.shape[0]
    row = pl.BlockSpec((ROW_BLOCK, D_MODEL), lambda i: (i, 0))
    return pl.pallas_call(
        _fox_out_kernel, grid=(n // ROW_BLOCK,),
        in_specs=[row, row, _resident((D_MODEL, D_MODEL))], out_specs=row,
        out_shape=jax.ShapeDtypeStruct((n, D_MODEL), F32),
        compiler_params=_params(1), name="fox_out",
    )(x, o, wo)


def _fox_mixer(x, bsz, seqlen, gain, w_in, b_f, wo):
    wq, wk, wv = (w_in[:, i * D_MODEL:(i + 1) * D_MODEL].astype(BF16) for i in range(3))
    wf = jnp.pad(w_in[:, 3 * D_MODEL:], ((0, 0), (0, LANES - FOX_HEADS))).astype(BF16)
    bf = jnp.pad(b_f, (0, LANES - FOX_HEADS)).reshape(1, LANES)
    q, k, v, cum = _fox_proj(x, bsz, seqlen, gain, wq, wk, wv, wf, bf)
    cum = cum.reshape(bsz, seqlen, LANES)
    cum_t = cum[:, :, :FOX_HEADS].transpose(0, 2, 1)
    shape = (bsz, seqlen, D_MODEL)
    o = _fox_attn(q.reshape(shape), k.reshape(shape), v.reshape(shape), cum, cum_t)
    return _fox_out(x, o.reshape(bsz * seqlen, D_MODEL), wo.astype(BF16))


def kernel(x, p, norm_mix, norm_ffn, norm_ple, norm_final, ssm_lam_re, ssm_lam_im, ssm_log_dt,
           ssm_b_re, ssm_b_im, ssm_c_re, ssm_c_im, ssm_d, ssm_w_glu, conv_w_in, conv_w,
           conv_w_out, fox_w_in, fox_b_f, fox_w_out, mlp_w1, mlp_w2, ple_w, ple_gate_w):
    bsz, seqlen, _ = x.shape
    depth = p.shape[0]
    assert seqlen % ROW_BLOCK == 0 and seqlen % ATTN_Q_BLOCK == 0 and bsz % SUBLANES == 0
    n = bsz * seqlen
    row_vec = lambda v: v.reshape(1, D_MODEL)
    x = x.reshape(n, D_MODEL)
    for i in range(depth):
        kind, slot = i % N_MIXERS, i // N_MIXERS
        gain = row_vec(norm_mix[i])
        if kind == 0:
            weights = _s5_weights(ssm_lam_re[slot], ssm_lam_im[slot], ssm_log_dt[slot],
                                  ssm_b_re[slot], ssm_b_im[slot], ssm_c_re[slot], ssm_c_im[slot])
            w_glu = ssm_w_glu[slot].astype(BF16)
            x = _s5_mixer(x, bsz, seqlen, gain, row_vec(ssm_d[slot]), weights,
                          w_glu[:, :D_MODEL], w_glu[:, D_MODEL:])
        elif kind == 1:
            w_in = conv_w_in[slot].astype(BF16)
            x = _conv_mixer(x, bsz, seqlen, gain, w_in[:, :D_MODEL],
                            w_in[:, D_MODEL:2 * D_MODEL], w_in[:, 2 * D_MODEL:],
                            conv_w[slot], conv_w_out[slot].astype(BF16))
        else:
            x = _fox_mixer(x, bsz, seqlen, gain, fox_w_in[slot], fox_b_f[slot], fox_w_out[slot])
        x = _ffn_ple(x, p[i].reshape(n, PLE_DIM), row_vec(norm_ffn[i]), row_vec(norm_ple[i]),
                     row_vec(norm_final), mlp_w1[i].astype(BF16), mlp_w2[i].astype(BF16),
                     ple_gate_w[i].astype(BF16), ple_w[i].astype(BF16),
                     final_norm=(i == depth - 1))
    return x.reshape(bsz, seqlen, D_MODEL)
```

```python
import functools
import math

import jax
import jax.numpy as jnp
from jax import lax
from jax.experimental import pallas as pl
from jax.experimental.pallas import tpu as pltpu

D_MODEL = 1024
D_FF = 4 * D_MODEL
PLE_DIM = 256
NORM_EPS = 1e-6
N_MIXERS = 3
SSM_GROUP = 16
SSM_GROUPS = D_MODEL // SSM_GROUP
SSM_STATE = 64
CONV_WIDTH = 3
FOX_HEAD_DIM = 64
FOX_HEADS = D_MODEL // FOX_HEAD_DIM
LANES = 128
SUBLANES = 8
VMEM_LIMIT = 56 * 1024 * 1024

ROW_BLOCK = 512
FF_CHUNK = 512
ATTN_Q_BLOCK = 256

SSM_STEP = 8
SSM_FOLD = SSM_STEP * LANES
SLABS = D_MODEL // LANES
SLAB_GROUPS = LANES // SSM_GROUP
SLAB_STATE = SLAB_GROUPS * SSM_STATE
SCAN_TIME_BLOCK = 1024
INTERLEAVE_TIME_BLOCK = 128
GLU_TIME_BLOCK = ROW_BLOCK // SUBLANES

BF16 = jnp.bfloat16
F32 = jnp.float32


def _rms(x, gain):
    return x * lax.rsqrt(jnp.mean(x * x, axis=-1, keepdims=True) + NORM_EPS) * gain


def _dot(a, b):
    return jnp.dot(a, b, preferred_element_type=F32)


def _params(n_axes):
    return pltpu.CompilerParams(dimension_semantics=("arbitrary",) * n_axes,
                                vmem_limit_bytes=VMEM_LIMIT)


def _resident(shape):
    nd = len(shape)
    return pl.BlockSpec(shape, lambda *_: (0,) * nd, pipeline_mode=pl.Buffered(1))


def _ffn_ple_kernel(x_ref, p_ref, gf_ref, gp_ref, gl_ref, w1_ref, w2_ref, wg_ref, wp_ref,
                    o_ref, acc_ref, *, final_norm):
    x = x_ref[...]
    h = _rms(x, gf_ref[...]).astype(BF16)
    acc_ref[...] = jnp.zeros_like(acc_ref)

    def chunk(c, carry):
        col = pl.multiple_of(c * FF_CHUNK, FF_CHUNK)
        a = jnp.maximum(_dot(h, w1_ref[:, pl.ds(col, FF_CHUNK)]), 0.0)
        acc_ref[...] += _dot((a * a).astype(BF16), w2_ref[pl.ds(col, FF_CHUNK), :])
        return carry

    lax.fori_loop(0, D_FF // FF_CHUNK, chunk, 0)
    x = x + acc_ref[...]
    gate = jax.nn.sigmoid(_dot(_rms(x, gp_ref[...]).astype(BF16), wg_ref[...]))
    x = x + _dot(p_ref[...].astype(BF16), wp_ref[...]) * gate
    if final_norm:
        x = _rms(x, gl_ref[...])
    o_ref[...] = x


def _ffn_ple(x, p, g_ffn, g_ple, g_last, w1, w2, wg, wp, final_norm):
    n = x.shape[0]
    row = lambda w: pl.BlockSpec((ROW_BLOCK, w), lambda i: (i, 0))
    return pl.pallas_call(
        functools.partial(_ffn_ple_kernel, final_norm=final_norm),
        grid=(n // ROW_BLOCK,),
        in_specs=[row(D_MODEL), row(PLE_DIM),
                  _resident((1, D_MODEL)), _resident((1, D_MODEL)), _resident((1, D_MODEL)),
                  _resident((D_MODEL, D_FF)), _resident((D_FF, D_MODEL)),
                  _resident((D_MODEL, D_MODEL)), _resident((PLE_DIM, D_MODEL))],
        out_specs=row(D_MODEL),
        out_shape=jax.ShapeDtypeStruct((n, D_MODEL), F32),
        scratch_shapes=[pltpu.VMEM((ROW_BLOCK, D_MODEL), F32)],
        compiler_params=_params(1),
        name="ffn_ple",
    )(x, p, g_ffn, g_ple, g_last, w1, w2, wg, wp)


def _s5_interleave_kernel(x_ref, g_ref, o_ref):
    for b in range(SUBLANES):
        o_ref[:, b, :] = _rms(x_ref[b], g_ref[...])


def _s5_interleave(x4, gain):
    groups, _, seqlen, _ = x4.shape
    tb = INTERLEAVE_TIME_BLOCK
    return pl.pallas_call(
        _s5_interleave_kernel, grid=(groups, seqlen // tb),
        in_specs=[pl.BlockSpec((None, SUBLANES, tb, D_MODEL), lambda g, t: (g, 0, t, 0)),
                  _resident((1, D_MODEL))],
        out_specs=pl.BlockSpec((None, tb, SUBLANES, D_MODEL), lambda g, t: (g, t, 0, 0)),
        out_shape=jax.ShapeDtypeStruct((groups, seqlen, SUBLANES, D_MODEL), F32),
        compiler_params=_params(2), name="s5_interleave",
    )(x4, gain)


def _s5_scan_kernel(u_ref, bs_ref, cs_ref, ds_ref, a_ref, y_ref, bu_ref, xs_ref, st_ref):
    steps = u_ref.shape[0]
    rows = steps * SUBLANES

    @pl.when(pl.program_id(2) == 0)
    def _():
        st_ref[...] = jnp.zeros_like(st_ref)

    u = jnp.concatenate([u_ref[:, i].reshape(rows, LANES).astype(BF16)
                         for i in range(SSM_STEP)], axis=1)
    bu_ref[...] = _dot(u, bs_ref[...])
    a_re = jnp.broadcast_to(a_ref[0:1, :], (SUBLANES, SLAB_STATE))
    a_im = jnp.broadcast_to(a_ref[1:2, :], (SUBLANES, SLAB_STATE))

    def step(k, carry):
        s_re, s_im = carry
        r0 = pl.multiple_of(k * SUBLANES, SUBLANES)
        xs_ref[pl.ds(r0, SUBLANES), 0:SLAB_STATE] = s_re
        xs_ref[pl.ds(r0, SUBLANES), SLAB_STATE:] = s_im
        b_re = bu_ref[pl.ds(r0, SUBLANES), 0:SLAB_STATE]
        b_im = bu_ref[pl.ds(r0, SUBLANES), SLAB_STATE:]
        return (a_re * s_re - a_im * s_im + b_re, a_re * s_im + a_im * s_re + b_im)

    s_re, s_im = lax.fori_loop(0, steps, step, (st_ref[0], st_ref[1]))
    st_ref[0] = s_re
    st_ref[1] = s_im
    y = _dot(xs_ref[...].astype(BF16), cs_ref[...]) + _dot(u, ds_ref[...])
    for j in range(SSM_STEP):
        y_ref[:, j] = y[:, j * LANES:(j + 1) * LANES].reshape(steps, SUBLANES, LANES)


def _s5_scan(u5, bs, cs, ds, a_pow):
    groups, total_steps = u5.shape[0], u5.shape[1]
    steps = min(SCAN_TIME_BLOCK // SSM_STEP, total_steps)
    rows = steps * SUBLANES
    act = pl.BlockSpec((None, steps, SSM_STEP, SUBLANES, LANES), lambda s, g, t: (g, t, 0, 0, s))
    per_slab = lambda *shape: pl.BlockSpec((None,) + shape, lambda s, g, t: (s,) + (0,) * len(shape))
    return pl.pallas_call(
        _s5_scan_kernel, grid=(SLABS, groups, total_steps // steps),
        in_specs=[act, per_slab(SSM_FOLD, 2 * SLAB_STATE), per_slab(2 * SLAB_STATE, SSM_FOLD),
                  per_slab(SSM_FOLD, SSM_FOLD), per_slab(2, SLAB_STATE)],
        out_specs=act, out_shape=jax.ShapeDtypeStruct(u5.shape, F32),
        scratch_shapes=[pltpu.VMEM((rows, 2 * SLAB_STATE), F32),
                        pltpu.VMEM((rows, 2 * SLAB_STATE), F32),
                        pltpu.VMEM((2, SUBLANES, SLAB_STATE), F32)],
        compiler_params=_params(3), name="s5_scan",
    )(u5, bs, cs, ds, a_pow)


def _s5_glu_kernel(x_ref, y_ref, g_ref, d_ref, wa_ref, wg_ref, o_ref, ybuf_ref):
    tb = x_ref.shape[1]
    for b in range(SUBLANES):
        ybuf_ref[b * tb:(b + 1) * tb, :] = y_ref[:, b, :]
    x = x_ref[...].reshape(SUBLANES * tb, D_MODEL)
    u = _rms(x, g_ref[...])
    z = jax.nn.gelu(ybuf_ref[...] + d_ref[...] * u).astype(BF16)
    out = x + _dot(z, wa_ref[...]) * jax.nn.sigmoid(_dot(z, wg_ref[...]))
    o_ref[...] = out.reshape(SUBLANES, tb, D_MODEL)


def _s5_glu(x4, y4, gain, d_skip, wa, wg):
    groups, _, seqlen, _ = x4.shape
    tb = GLU_TIME_BLOCK
    nat = pl.BlockSpec((None, SUBLANES, tb, D_MODEL), lambda g, t: (g, 0, t, 0))
    return pl.pallas_call(
        _s5_glu_kernel, grid=(groups, seqlen // tb),
        in_specs=[nat, pl.BlockSpec((None, tb, SUBLANES, D_MODEL), lambda g, t: (g, t, 0, 0)),
                  _resident((1, D_MODEL)), _resident((1, D_MODEL)),
                  _resident((D_MODEL, D_MODEL)), _resident((D_MODEL, D_MODEL))],
        out_specs=nat, out_shape=jax.ShapeDtypeStruct(x4.shape, F32),
        scratch_shapes=[pltpu.VMEM((SUBLANES * tb, D_MODEL), F32)],
        compiler_params=_params(2), name="s5_glu",
    )(x4, y4, gain, d_skip, wa, wg)


def _s5_weights(lam_re, lam_im, log_dt, b_re, b_im, c_re, c_im):
    hp = lax.Precision.HIGHEST
    S, G, H, P = SSM_STEP, SSM_GROUPS, SSM_GROUP, SSM_STATE
    dt = jnp.exp(log_dt)[:, None]
    def powers(n):
        n = n.astype(F32)[:, None, None]
        mag = jnp.exp(n * (lam_re * dt))
        return mag * jnp.cos(n * (lam_im * dt)), mag * jnp.sin(n * (lam_im * dt))

    pw_re, pw_im = powers(jnp.arange(S + 1))
    rev_re, rev_im = powers(S - 1 - jnp.arange(S))
    nr, ni = pw_re[1] - 1.0, pw_im[1]
    den = lam_re * lam_re + lam_im * lam_im
    coef_re = (nr * lam_re + ni * lam_im) / den
    coef_im = (ni * lam_re - nr * lam_im) / den
    bb_re = coef_re[..., None] * b_re - coef_im[..., None] * b_im
    bb_im = coef_re[..., None] * b_im + coef_im[..., None] * b_re
    ca_re = c_re[None] * pw_re[:, :, None, :] - c_im[None] * pw_im[:, :, None, :]
    ca_im = c_re[None] * pw_im[:, :, None, :] + c_im[None] * pw_re[:, :, None, :]
    bs_re = rev_re[:, :, :, None] * bb_re[None] - rev_im[:, :, :, None] * bb_im[None]
    bs_im = rev_re[:, :, :, None] * bb_im[None] + rev_im[:, :, :, None] * bb_re[None]
    kern = (jnp.einsum('tghp,gpk->tghk', ca_re[:S], bb_re, precision=hp)
            - jnp.einsum('tghp,gpk->tghk', ca_im[:S], bb_im, precision=hp))
    lag = jnp.arange(S)[None, :] - jnp.arange(S)[:, None]
    toep = jnp.where((lag >= 0)[:, :, None, None, None], kern[jnp.maximum(lag, 0)], 0.0)
    eye = jnp.eye(SLAB_GROUPS, dtype=F32)
    slab = lambda t, axis: t.reshape(t.shape[:axis] + (SLABS, SLAB_GROUPS) + t.shape[axis + 1:])
    bs = jnp.stack([bs_re, bs_im], axis=0)
    bs = jnp.einsum('cisgph,gk->sighckp', slab(bs, 2), eye)
    bs = bs.reshape(SLABS, SSM_FOLD, 2 * SLAB_STATE)
    cs = jnp.stack([ca_re[1:], -ca_im[1:]], axis=0)
    cs = jnp.einsum('cjsghp,gk->scgpjkh', slab(cs, 2), eye)
    cs = cs.reshape(SLABS, 2 * SLAB_STATE, SSM_FOLD)
    ds = jnp.einsum('ijsghk,gm->sigkjmh', slab(toep, 2), eye)
    ds = ds.reshape(SLABS, SSM_FOLD, SSM_FOLD)
    a_pow = jnp.stack([pw_re[S].reshape(SLABS, SLAB_STATE),
                       pw_im[S].reshape(SLABS, SLAB_STATE)], axis=1)
    return bs.astype(BF16), cs.astype(BF16), ds.astype(BF16), a_pow


def _s5_mixer(x, bsz, seqlen, gain, d_skip, weights, wa, wg):
    groups = bsz // SUBLANES
    x4 = x.reshape(groups, SUBLANES, seqlen, D_MODEL)
    h = _s5_interleave(x4, gain)
    folded = (groups, seqlen // SSM_STEP, SSM_STEP, SUBLANES, D_MODEL)
    y = _s5_scan(h.reshape(folded), *weights)
    out = _s5_glu(x4, y.reshape(h.shape), gain, d_skip, wa, wg)
    return out.reshape(bsz * seqlen, D_MODEL)


def _conv_kernel(x_ref, g_ref, wb_ref, wc_ref, wv_ref, cw_ref, wo_ref, o_ref, z_ref):
    t = pl.program_id(1)
    rows = x_ref.shape[0]

    @pl.when(t == 0)
    def _():
        z_ref[0:SUBLANES, :] = jnp.zeros((SUBLANES, D_MODEL), F32)

    x = x_ref[...]
    h = _rms(x, g_ref[...]).astype(BF16)
    z_ref[SUBLANES:, :] = _dot(h, wc_ref[...]) * _dot(h, wv_ref[...])
    conv = cw_ref[CONV_WIDTH - 1:CONV_WIDTH, :] * z_ref[SUBLANES:, :]
    for tap in range(CONV_WIDTH - 1):
        back = CONV_WIDTH - 1 - tap
        conv = conv + cw_ref[tap:tap + 1, :] * z_ref[SUBLANES - back:SUBLANES - back + rows, :]
    z_ref[0:SUBLANES, :] = z_ref[rows:rows + SUBLANES, :]
    gated = (_dot(h, wb_ref[...]) * conv).astype(BF16)
    o_ref[...] = x + _dot(gated, wo_ref[...])


def _conv_mixer(x, bsz, seqlen, gain, wb, wc, wv, conv_w, wo):
    tblocks = seqlen // ROW_BLOCK
    row = pl.BlockSpec((ROW_BLOCK, D_MODEL), lambda b, t: (b * tblocks + t, 0))
    sq = _resident((D_MODEL, D_MODEL))
    return pl.pallas_call(
        _conv_kernel, grid=(bsz, tblocks),
        in_specs=[row, _resident((1, D_MODEL)), sq, sq, sq,
                  _resident((CONV_WIDTH, D_MODEL)), sq],
        out_specs=row, out_shape=jax.ShapeDtypeStruct(x.shape, F32),
        scratch_shapes=[pltpu.VMEM((ROW_BLOCK + SUBLANES, D_MODEL), F32)],
        compiler_params=_params(2), name="conv_mixer",
    )(x, gain, wb, wc, wv, conv_w, wo)


def _fox_proj_kernel(x_ref, g_ref, wq_ref, wk_ref, wv_ref, wf_ref, bf_ref,
                     q_ref, k_ref, v_ref, cum_ref, carry_ref):
    t = pl.program_id(1)
    rows = x_ref.shape[0]

    @pl.when(t == 0)
    def _():
        carry_ref[...] = jnp.zeros_like(carry_ref)

    h = _rms(x_ref[...], g_ref[...]).astype(BF16)
    q_ref[...] = _dot(h, wq_ref[...]).astype(BF16)
    k_ref[...] = _dot(h, wk_ref[...]).astype(BF16)
    v_ref[...] = _dot(h, wv_ref[...]).astype(BF16)
    logit = _dot(h, wf_ref[...]) + bf_ref[...]
    log_f = jnp.minimum(logit, 0.0) - jnp.log1p(jnp.exp(-jnp.abs(logit)))
    tri = (lax.broadcasted_iota(jnp.int32, (rows, rows), 0)
           >= lax.broadcasted_iota(jnp.int32, (rows, rows), 1)).astype(BF16)
    hi = log_f.astype(BF16)
    rest = log_f - hi.astype(F32)
    mid = rest.astype(BF16)
    lo = (rest - mid.astype(F32)).astype(BF16)
    cum = _dot(tri, hi) + _dot(tri, mid) + _dot(tri, lo) + carry_ref[0:1, :]
    cum_ref[...] = cum
    carry_ref[0:1, :] = cum[rows - 1:rows, :]


def _fox_proj(x, bsz, seqlen, gain, wq, wk, wv, wf, bf):
    tblocks = seqlen // ROW_BLOCK
    idx = lambda b, t: (b * tblocks + t, 0)
    row = pl.BlockSpec((ROW_BLOCK, D_MODEL), idx)
    sq = _resident((D_MODEL, D_MODEL))
    qkv = jax.ShapeDtypeStruct(x.shape, BF16)
    return pl.pallas_call(
        _fox_proj_kernel, grid=(bsz, tblocks),
        in_specs=[row, _resident((1, D_MODEL)), sq, sq, sq,
                  _resident((D_MODEL, LANES)), _resident((1, LANES))],
        out_specs=[row, row, row, pl.BlockSpec((ROW_BLOCK, LANES), idx)],
        out_shape=[qkv, qkv, qkv, jax.ShapeDtypeStruct((x.shape[0], LANES), F32)],
        scratch_shapes=[pltpu.VMEM((SUBLANES, LANES), F32)],
        compiler_params=_params(2), name="fox_proj",
    )(x, gain, wq, wk, wv, wf, bf)


def _fox_attn_kernel(q_ref, k_ref, v_ref, cq_ref, ck_ref, o_ref):
    pair = pl.program_id(1)
    seqlen = q_ref.shape[1]
    tq = ATTN_Q_BLOCK
    lane = lax.broadcasted_iota(jnp.int32, (1, LANES), 1)
    low = lane < FOX_HEAD_DIM
    scale = FOX_HEAD_DIM ** -0.5
    causal = (lax.broadcasted_iota(jnp.int32, (tq, tq), 0)
              >= lax.broadcasted_iota(jnp.int32, (tq, tq), 1))
    contract_last = (((1,), (1,)), ((), ()))
    for qb in range(seqlen // tq):
        q0 = qb * tq
        q = q_ref[0, q0:q0 + tq, :] * jnp.asarray(scale, BF16)
        cq_blk = cq_ref[0, q0:q0 + tq, :]
        outs = []
        for e in range(2):
            head = 2 * pair + e
            mine = low if e == 0 else jnp.logical_not(low)
            qm = jnp.where(mine, q, jnp.zeros_like(q))
            cq = jnp.sum(jnp.where(lane == head, cq_blk, 0.0), axis=1, keepdims=True)
            ck = ck_ref[0, pl.ds(head, 1), 0:q0 + tq]
            s_d = lax.dot_general(qm, k_ref[0, q0:q0 + tq, :], contract_last,
                                  preferred_element_type=F32)
            s_d = jnp.where(causal, s_d + (cq - ck[:, q0:q0 + tq]), -jnp.inf)
            m = jnp.max(s_d, axis=1, keepdims=True)
            if qb > 0:
                s_p = lax.dot_general(qm, k_ref[0, 0:q0, :], contract_last,
                                      preferred_element_type=F32)
                s_p = s_p + (cq - ck[:, 0:q0])
                m = jnp.maximum(m, jnp.max(s_p, axis=1, keepdims=True))
            p_d = jnp.exp(s_d - m)
            denom = jnp.sum(p_d, axis=1, keepdims=True)
            acc = _dot(p_d.astype(BF16), v_ref[0, q0:q0 + tq, :])
            if qb > 0:
                p_p = jnp.exp(s_p - m)
                denom = denom + jnp.sum(p_p, axis=1, keepdims=True)
                acc = acc + _dot(p_p.astype(BF16), v_ref[0, 0:q0, :])
            outs.append(acc / denom)
        o_ref[0, q0:q0 + tq, :] = jnp.where(low, outs[0], outs[1]).astype(BF16)


def _fox_attn(q, k, v, cum, cum_t):
    bsz, seqlen, _ = q.shape
    seq = pl.BlockSpec((1, seqlen, LANES), lambda b, j: (b, 0, j))
    return pl.pallas_call(
        _fox_attn_kernel, grid=(bsz, FOX_HEADS // 2),
        in_specs=[seq, seq, seq,
                  pl.BlockSpec((1, seqlen, LANES), lambda b, j: (b, 0, 0)),
                  pl.BlockSpec((1, FOX_HEADS, seqlen), lambda b, j: (b, 0, 0))],
        out_specs=seq, out_shape=jax.ShapeDtypeStruct(q.shape, BF16),
        compiler_params=_params(2), name="fox_attn",
    )(q, k, v, cum, cum_t)


def _fox_out_kernel(x_ref, o_ref, wo_ref, y_ref):
    y_ref[...] = x_ref[...] + _dot(o_ref[...], wo_ref[...])


def _fox_out(x, o, wo):
    n = x.shape[0]
    row = pl.BlockSpec((ROW_BLOCK, D_MODEL), lambda i: (i, 0))
    return pl.pallas_call(
        _fox_out_kernel, grid=(n // ROW_BLOCK,),
        in_specs=[row, row, _resident((D_MODEL, D_MODEL))], out_specs=row,
        out_shape=jax.ShapeDtypeStruct((n, D_MODEL), F32),
        compiler_params=_params(1), name="fox_out",
    )(x, o, wo)


def _fox_mixer(x, bsz, seqlen, gain, w_in, b_f, wo):
    wq, wk, wv = (w_in[:, i * D_MODEL:(i + 1) * D_MODEL].astype(BF16) for i in range(3))
    wf = jnp.pad(w_in[:, 3 * D_MODEL:], ((0, 0), (0, LANES - FOX_HEADS))).astype(BF16)
    bf = jnp.pad(b_f, (0, LANES - FOX_HEADS)).reshape(1, LANES)
    q, k, v, cum = _fox_proj(x, bsz, seqlen, gain, wq, wk, wv, wf, bf)
    cum = cum.reshape(bsz, seqlen, LANES)
    cum_t = cum[:, :, :FOX_HEADS].transpose(0, 2, 1)
    shape = (bsz, seqlen, D_MODEL)
    o = _fox_attn(q.reshape(shape), k.reshape(shape), v.reshape(shape), cum, cum_t)
    return _fox_out(x, o.reshape(bsz * seqlen, D_MODEL), wo.astype(BF16))


def kernel(x, p, norm_mix, norm_ffn, norm_ple, norm_final, ssm_lam_re, ssm_lam_im, ssm_log_dt,
           ssm_b_re, ssm_b_im, ssm_c_re, ssm_c_im, ssm_d, ssm_w_glu, conv_w_in, conv_w,
           conv_w_out, fox_w_in, fox_b_f, fox_w_out, mlp_w1, mlp_w2, ple_w, ple_gate_w):
    bsz, seqlen, _ = x.shape
    depth = p.shape[0]
    assert seqlen % ROW_BLOCK == 0 and seqlen % ATTN_Q_BLOCK == 0 and bsz % SUBLANES == 0
    n = bsz * seqlen
    row_vec = lambda v: v.reshape(1, D_MODEL)
    x = x.reshape(n, D_MODEL)
    for i in range(depth):
        kind, slot = i % N_MIXERS, i // N_MIXERS
        gain = row_vec(norm_mix[i])
        if kind == 0:
            weights = _s5_weights(ssm_lam_re[slot], ssm_lam_im[slot], ssm_log_dt[slot],
                                  ssm_b_re[slot], ssm_b_im[slot], ssm_c_re[slot], ssm_c_im[slot])
            w_glu = ssm_w_glu[slot].astype(BF16)
            x = _s5_mixer(x, bsz, seqlen, gain, row_vec(ssm_d[slot]), weights,
                          w_glu[:, :D_MODEL], w_glu[:, D_MODEL:])
        elif kind == 1:
            w_in = conv_w_in[slot].astype(BF16)
            x = _conv_mixer(x, bsz, seqlen, gain, w_in[:, :D_MODEL],
                            w_in[:, D_MODEL:2 * D_MODEL], w_in[:, 2 * D_MODEL:],
                            conv_w[slot], conv_w_out[slot].astype(BF16))
        else:
            x = _fox_mixer(x, bsz, seqlen, gain, fox_w_in[slot], fox_b_f[slot], fox_w_out[slot])
        x = _ffn_ple(x, p[i].reshape(n, PLE_DIM), row_vec(norm_ffn[i]), row_vec(norm_ple[i]),
                     row_vec(norm_final), mlp_w1[i].astype(BF16), mlp_w2[i].astype(BF16),
                     ple_gate_w[i].astype(BF16), ple_w[i].astype(BF16),
                     final_norm=(i == depth - 1))
    return x.reshape(bsz, seqlen, D_MODEL)
```

```python
import functools
import math

import jax
import jax.numpy as jnp
from jax import lax
from jax.experimental import pallas as pl
from jax.experimental.pallas import tpu as pltpu

D_MODEL = 1024
D_FF = 4 * D_MODEL
PLE_DIM = 256
NORM_EPS = 1e-6
N_MIXERS = 3
SSM_GROUP = 16
SSM_GROUPS = D_MODEL // SSM_GROUP
SSM_STATE = 64
CONV_WIDTH = 3
FOX_HEAD_DIM = 64
FOX_HEADS = D_MODEL // FOX_HEAD_DIM
FOX_BIAS_LANES = 6
LANES = 128
SUBLANES = 8
VMEM_LIMIT = 56 * 1024 * 1024

ROW_BLOCK = 512
FFN_ROW_BLOCK = 1024
FF_CHUNK = 512
ATTN_Q_BLOCK = 256
ATTN_SLOTS = 4

SSM_STEP = 8
SSM_FOLD = SSM_STEP * LANES
SLABS = D_MODEL // LANES
SLAB_GROUPS = LANES // SSM_GROUP
SLAB_STATE = SLAB_GROUPS * SSM_STATE
SCAN_TIME_BLOCK = 1024
INTERLEAVE_TIME_BLOCK = 128
GLU_TIME_BLOCK = ROW_BLOCK // SUBLANES

BF16 = jnp.bfloat16
F32 = jnp.float32


def _rms(x, gain):
    return x * lax.rsqrt(jnp.mean(x * x, axis=-1, keepdims=True) + NORM_EPS) * gain


def _dot(a, b):
    return jnp.dot(a, b, preferred_element_type=F32)


def _params(n_axes):
    return pltpu.CompilerParams(dimension_semantics=("arbitrary",) * n_axes,
                                vmem_limit_bytes=VMEM_LIMIT)


def _resident(shape):
    nd = len(shape)
    return pl.BlockSpec(shape, lambda *_: (0,) * nd, pipeline_mode=pl.Buffered(1))


def _ffn_ple_kernel(x_ref, p_ref, gf_ref, gp_ref, gl_ref, w1_ref, w2_ref, wg_ref, wp_ref,
                    o_ref, h_ref, *, final_norm):
    h_ref[...] = _rms(x_ref[...], gf_ref[...]).astype(BF16)
    o_ref[...] = x_ref[...]

    def chunk(c, carry):
        col = pl.multiple_of(c * FF_CHUNK, FF_CHUNK)
        a = jnp.maximum(_dot(h_ref[...], w1_ref[:, pl.ds(col, FF_CHUNK)]), 0.0)
        o_ref[...] += _dot((a * a).astype(BF16), w2_ref[pl.ds(col, FF_CHUNK), :])
        return carry

    lax.fori_loop(0, D_FF // FF_CHUNK, chunk, 0)
    x = o_ref[...]
    gate = jax.nn.sigmoid(_dot(_rms(x, gp_ref[...]).astype(BF16), wg_ref[...]))
    x = x + _dot(p_ref[...].astype(BF16), wp_ref[...]) * gate
    if final_norm:
        x = _rms(x, gl_ref[...])
    o_ref[...] = x


def _ffn_ple(x, p, g_ffn, g_ple, g_last, w1, w2, wg, wp, final_norm):
    n = x.shape[0]
    row = lambda w: pl.BlockSpec((FFN_ROW_BLOCK, w), lambda i: (i, 0))
    return pl.pallas_call(
        functools.partial(_ffn_ple_kernel, final_norm=final_norm),
        grid=(n // FFN_ROW_BLOCK,),
        in_specs=[row(D_MODEL), row(PLE_DIM),
                  _resident((1, D_MODEL)), _resident((1, D_MODEL)), _resident((1, D_MODEL)),
                  _resident((D_MODEL, D_FF)), _resident((D_FF, D_MODEL)),
                  _resident((D_MODEL, D_MODEL)), _resident((PLE_DIM, D_MODEL))],
        out_specs=row(D_MODEL),
        out_shape=jax.ShapeDtypeStruct((n, D_MODEL), F32),
        scratch_shapes=[pltpu.VMEM((FFN_ROW_BLOCK, D_MODEL), BF16)],
        compiler_params=_params(1),
        name="ffn_ple",
    )(x, p, g_ffn, g_ple, g_last, w1, w2, wg, wp)


def _s5_interleave_kernel(x_ref, g_ref, o_ref):
    for b in range(SUBLANES):
        o_ref[:, b, :] = _rms(x_ref[b], g_ref[...])


def _s5_interleave(x4, gain):
    groups, _, seqlen, _ = x4.shape
    tb = INTERLEAVE_TIME_BLOCK
    return pl.pallas_call(
        _s5_interleave_kernel, grid=(groups, seqlen // tb),
        in_specs=[pl.BlockSpec((None, SUBLANES, tb, D_MODEL), lambda g, t: (g, 0, t, 0)),
                  _resident((1, D_MODEL))],
        out_specs=pl.BlockSpec((None, tb, SUBLANES, D_MODEL), lambda g, t: (g, t, 0, 0)),
        out_shape=jax.ShapeDtypeStruct((groups, seqlen, SUBLANES, D_MODEL), F32),
        compiler_params=_params(2), name="s5_interleave",
    )(x4, gain)


def _s5_scan_kernel(u_ref, bs_ref, cs_ref, ds_ref, a_ref, y_ref, bu_ref, xs_ref, st_ref):
    steps = u_ref.shape[0]
    rows = steps * SUBLANES

    @pl.when(pl.program_id(2) == 0)
    def _():
        st_ref[...] = jnp.zeros_like(st_ref)

    u = jnp.concatenate([u_ref[:, i].reshape(rows, LANES).astype(BF16)
                         for i in range(SSM_STEP)], axis=1)
    bu_ref[...] = _dot(u, bs_ref[...])
    a_re = jnp.broadcast_to(a_ref[0:1, :], (SUBLANES, SLAB_STATE))
    a_im = jnp.broadcast_to(a_ref[1:2, :], (SUBLANES, SLAB_STATE))

    def step(k, carry):
        s_re, s_im = carry
        r0 = pl.multiple_of(k * SUBLANES, SUBLANES)
        xs_ref[pl.ds(r0, SUBLANES), 0:SLAB_STATE] = s_re
        xs_ref[pl.ds(r0, SUBLANES), SLAB_STATE:] = s_im
        b_re = bu_ref[pl.ds(r0, SUBLANES), 0:SLAB_STATE]
        b_im = bu_ref[pl.ds(r0, SUBLANES), SLAB_STATE:]
        return (a_re * s_re - a_im * s_im + b_re, a_re * s_im + a_im * s_re + b_im)

    s_re, s_im = lax.fori_loop(0, steps, step, (st_ref[0], st_ref[1]))
    st_ref[0] = s_re
    st_ref[1] = s_im
    y = _dot(xs_ref[...].astype(BF16), cs_ref[...]) + _dot(u, ds_ref[...])
    for j in range(SSM_STEP):
        y_ref[:, j] = y[:, j * LANES:(j + 1) * LANES].reshape(steps, SUBLANES, LANES)


def _s5_scan(u5, bs, cs, ds, a_pow):
    groups, total_steps = u5.shape[0], u5.shape[1]
    steps = min(SCAN_TIME_BLOCK // SSM_STEP, total_steps)
    rows = steps * SUBLANES
    act = pl.BlockSpec((None, steps, SSM_STEP, SUBLANES, LANES), lambda s, g, t: (g, t, 0, 0, s))
    per_slab = lambda *shape: pl.BlockSpec((None,) + shape, lambda s, g, t: (s,) + (0,) * len(shape))
    return pl.pallas_call(
        _s5_scan_kernel, grid=(SLABS, groups, total_steps // steps),
        in_specs=[act, per_slab(SSM_FOLD, 2 * SLAB_STATE), per_slab(2 * SLAB_STATE, SSM_FOLD),
                  per_slab(SSM_FOLD, SSM_FOLD), per_slab(2, SLAB_STATE)],
        out_specs=act, out_shape=jax.ShapeDtypeStruct(u5.shape, F32),
        scratch_shapes=[pltpu.VMEM((rows, 2 * SLAB_STATE), F32),
                        pltpu.VMEM((rows, 2 * SLAB_STATE), F32),
                        pltpu.VMEM((2, SUBLANES, SLAB_STATE), F32)],
        compiler_params=_params(3), name="s5_scan",
    )(u5, bs, cs, ds, a_pow)


def _s5_glu_kernel(x_ref, y_ref, g_ref, d_ref, wa_ref, wg_ref, o_ref, ybuf_ref):
    tb = x_ref.shape[1]
    for b in range(SUBLANES):
        ybuf_ref[b * tb:(b + 1) * tb, :] = y_ref[:, b, :]
    x = x_ref[...].reshape(SUBLANES * tb, D_MODEL)
    u = _rms(x, g_ref[...])
    z = jax.nn.gelu(ybuf_ref[...] + d_ref[...] * u).astype(BF16)
    out = x + _dot(z, wa_ref[...]) * jax.nn.sigmoid(_dot(z, wg_ref[...]))
    o_ref[...] = out.reshape(SUBLANES, tb, D_MODEL)


def _s5_glu(x4, y4, gain, d_skip, wa, wg):
    groups, _, seqlen, _ = x4.shape
    tb = GLU_TIME_BLOCK
    nat = pl.BlockSpec((None, SUBLANES, tb, D_MODEL), lambda g, t: (g, 0, t, 0))
    return pl.pallas_call(
        _s5_glu_kernel, grid=(groups, seqlen // tb),
        in_specs=[nat, pl.BlockSpec((None, tb, SUBLANES, D_MODEL), lambda g, t: (g, t, 0, 0)),
                  _resident((1, D_MODEL)), _resident((1, D_MODEL)),
                  _resident((D_MODEL, D_MODEL)), _resident((D_MODEL, D_MODEL))],
        out_specs=nat, out_shape=jax.ShapeDtypeStruct(x4.shape, F32),
        scratch_shapes=[pltpu.VMEM((SUBLANES * tb, D_MODEL), F32)],
        compiler_params=_params(2), name="s5_glu",
    )(x4, y4, gain, d_skip, wa, wg)


def _s5_weights(lam_re, lam_im, log_dt, b_re, b_im, c_re, c_im):
    hp = lax.Precision.HIGHEST
    S, G, H, P = SSM_STEP, SSM_GROUPS, SSM_GROUP, SSM_STATE
    dt = jnp.exp(log_dt)[:, None]
    def powers(n):
        n = n.astype(F32)[:, None, None]
        mag = jnp.exp(n * (lam_re * dt))
        return mag * jnp.cos(n * (lam_im * dt)), mag * jnp.sin(n * (lam_im * dt))

    pw_re, pw_im = powers(jnp.arange(S + 1))
    rev_re, rev_im = powers(S - 1 - jnp.arange(S))
    nr, ni = pw_re[1] - 1.0, pw_im[1]
    den = lam_re * lam_re + lam_im * lam_im
    coef_re = (nr * lam_re + ni * lam_im) / den
    coef_im = (ni * lam_re - nr * lam_im) / den
    bb_re = coef_re[..., None] * b_re - coef_im[..., None] * b_im
    bb_im = coef_re[..., None] * b_im + coef_im[..., None] * b_re
    ca_re = c_re[None] * pw_re[:, :, None, :] - c_im[None] * pw_im[:, :, None, :]
    ca_im = c_re[None] * pw_im[:, :, None, :] + c_im[None] * pw_re[:, :, None, :]
    bs_re = rev_re[:, :, :, None] * bb_re[None] - rev_im[:, :, :, None] * bb_im[None]
    bs_im = rev_re[:, :, :, None] * bb_im[None] + rev_im[:, :, :, None] * bb_re[None]
    kern = (jnp.einsum('tghp,gpk->tghk', ca_re[:S], bb_re, precision=hp)
            - jnp.einsum('tghp,gpk->tghk', ca_im[:S], bb_im, precision=hp))
    lag = jnp.arange(S)[None, :] - jnp.arange(S)[:, None]
    toep = jnp.where((lag >= 0)[:, :, None, None, None], kern[jnp.maximum(lag, 0)], 0.0)
    same_group = jnp.eye(SLAB_GROUPS, dtype=F32)[None, None, :, None, None, :, None]

    def block_diag(t, shape):
        t = t.reshape(t.shape[:2] + (SLABS, SLAB_GROUPS) + t.shape[3:])
        t = t.transpose(2, 1, 3, 5, 0, 4)[:, :, :, :, :, None, :]
        return (t * same_group).astype(BF16).reshape((SLABS,) + shape)

    bs = block_diag(jnp.stack([bs_re, bs_im], axis=0), (SSM_FOLD, 2 * SLAB_STATE))
    cs = block_diag(jnp.stack([ca_re[1:], -ca_im[1:]], axis=1), (2 * SLAB_STATE, SSM_FOLD))
    ds = block_diag(toep.transpose(1, 0, 2, 3, 4), (SSM_FOLD, SSM_FOLD))
    a_pow = jnp.stack([pw_re[S].reshape(SLABS, SLAB_STATE),
                       pw_im[S].reshape(SLABS, SLAB_STATE)], axis=1)
    return bs, cs, ds, a_pow


def _s5_mixer(x, bsz, seqlen, gain, d_skip, weights, wa, wg):
    groups = bsz // SUBLANES
    x4 = x.reshape(groups, SUBLANES, seqlen, D_MODEL)
    h = _s5_interleave(x4, gain)
    folded = (groups, seqlen // SSM_STEP, SSM_STEP, SUBLANES, D_MODEL)
    y = _s5_scan(h.reshape(folded), *weights)
    out = _s5_glu(x4, y.reshape(h.shape), gain, d_skip, wa, wg)
    return out.reshape(bsz * seqlen, D_MODEL)


def _conv_kernel(x_ref, g_ref, wb_ref, wc_ref, wv_ref, cw_ref, wo_ref, o_ref, z_ref):
    t = pl.program_id(1)
    rows = x_ref.shape[0]

    @pl.when(t == 0)
    def _():
        z_ref[0:SUBLANES, :] = jnp.zeros((SUBLANES, D_MODEL), F32)

    x = x_ref[...]
    h = _rms(x, g_ref[...]).astype(BF16)
    z_ref[SUBLANES:, :] = _dot(h, wc_ref[...]) * _dot(h, wv_ref[...])
    conv = cw_ref[CONV_WIDTH - 1:CONV_WIDTH, :] * z_ref[SUBLANES:, :]
    for tap in range(CONV_WIDTH - 1):
        back = CONV_WIDTH - 1 - tap
        conv = conv + cw_ref[tap:tap + 1, :] * z_ref[SUBLANES - back:SUBLANES - back + rows, :]
    z_ref[0:SUBLANES, :] = z_ref[rows:rows + SUBLANES, :]
    gated = (_dot(h, wb_ref[...]) * conv).astype(BF16)
    o_ref[...] = x + _dot(gated, wo_ref[...])


def _conv_mixer(x, bsz, seqlen, gain, wb, wc, wv, conv_w, wo):
    tblocks = seqlen // ROW_BLOCK
    row = pl.BlockSpec((ROW_BLOCK, D_MODEL), lambda b, t: (b * tblocks + t, 0))
    sq = _resident((D_MODEL, D_MODEL))
    return pl.pallas_call(
        _conv_kernel, grid=(bsz, tblocks),
        in_specs=[row, _resident((1, D_MODEL)), sq, sq, sq,
                  _resident((CONV_WIDTH, D_MODEL)), sq],
        out_specs=row, out_shape=jax.ShapeDtypeStruct(x.shape, F32),
        scratch_shapes=[pltpu.VMEM((ROW_BLOCK + SUBLANES, D_MODEL), F32)],
        compiler_params=_params(2), name="conv_mixer",
    )(x, gain, wb, wc, wv, conv_w, wo)


def _split3(x):
    hi = x.astype(BF16)
    rest = x - hi.astype(F32)
    mid = rest.astype(BF16)
    return hi, mid, (rest - mid.astype(F32)).astype(BF16)


def _fox_proj_kernel(x_ref, g_ref, wq_ref, wk_ref, wv_ref, wf_ref, bf_ref,
                     q_ref, k_ref, v_ref, qb_ref, kb_ref, carry_ref):
    t = pl.program_id(1)
    rows = x_ref.shape[0]

    @pl.when(t == 0)
    def _():
        carry_ref[...] = jnp.zeros_like(carry_ref)

    h = _rms(x_ref[...], g_ref[...]).astype(BF16)
    q_ref[...] = _dot(h, wq_ref[...]).astype(BF16)
    k_ref[...] = _dot(h, wk_ref[...]).astype(BF16)
    v_ref[...] = _dot(h, wv_ref[...]).astype(BF16)
    logit = _dot(h, wf_ref[...]) + bf_ref[...]
    log_f = jnp.minimum(logit, 0.0) - jnp.log1p(jnp.exp(-jnp.abs(logit)))
    tri = (lax.broadcasted_iota(jnp.int32, (rows, rows), 0)
           >= lax.broadcasted_iota(jnp.int32, (rows, rows), 1)).astype(BF16)
    cum = sum(_dot(tri, piece) for piece in _split3(log_f)) + carry_ref[0:1, :]
    carry_ref[0:1, :] = cum[rows - 1:rows, :]
    src = lax.broadcasted_iota(jnp.int32, (LANES, LANES), 0)
    dst = lax.broadcasted_iota(jnp.int32, (LANES, LANES), 1)
    place = lambda j: ((dst == FOX_BIAS_LANES * src + j) & (src < FOX_HEADS)).astype(BF16)
    pieces = _split3(cum)
    lane = lax.broadcasted_iota(jnp.int32, (1, LANES), 1)
    slot = lane % FOX_BIAS_LANES
    used = lane < FOX_BIAS_LANES * FOX_HEADS
    ones_q = (used & (slot >= 3)).astype(F32)
    ones_k = (used & (slot < 3)).astype(F32)
    qb_ref[...] = (sum(_dot(pieces[j], place(j)) for j in range(3)) + ones_q).astype(BF16)
    kb_ref[...] = (ones_k - sum(_dot(pieces[j], place(3 + j)) for j in range(3))).astype(BF16)


def _fox_proj(x, bsz, seqlen, gain, wq, wk, wv, wf, bf):
    tblocks = seqlen // ROW_BLOCK
    idx = lambda b, t: (b * tblocks + t, 0)
    row = pl.BlockSpec((ROW_BLOCK, D_MODEL), idx)
    bias = pl.BlockSpec((ROW_BLOCK, LANES), idx)
    sq = _resident((D_MODEL, D_MODEL))
    qkv = jax.ShapeDtypeStruct(x.shape, BF16)
    bias_shape = jax.ShapeDtypeStruct((x.shape[0], LANES), BF16)
    return pl.pallas_call(
        _fox_proj_kernel, grid=(bsz, tblocks),
        in_specs=[row, _resident((1, D_MODEL)), sq, sq, sq,
                  _resident((D_MODEL, LANES)), _resident((1, LANES))],
        out_specs=[row, row, row, bias, bias],
        out_shape=[qkv, qkv, qkv, bias_shape, bias_shape],
        scratch_shapes=[pltpu.VMEM((SUBLANES, LANES), F32)],
        compiler_params=_params(2), name="fox_proj",
    )(x, gain, wq, wk, wv, wf, bf)


def _fold_rows(x, op):
    acc = x[0:SUBLANES]
    for r in range(SUBLANES, x.shape[0], SUBLANES):
        acc = op(acc, x[r:r + SUBLANES])
    return acc


def _fox_attn_kernel(q_ref, k_ref, v_ref, qb_ref, kb_ref, o_ref, kk_ref, vt_ref, s_ref, p_ref):
    pair = pl.program_id(1)
    seqlen = q_ref.shape[1]
    tq = tk = ATTN_Q_BLOCK
    lane = lax.broadcasted_iota(jnp.int32, (1, LANES), 1)
    low = lane < FOX_HEAD_DIM
    scale = FOX_HEAD_DIM ** -0.5
    keep = (lax.broadcasted_iota(jnp.int32, (tk, tq), 1)
            >= lax.broadcasted_iota(jnp.int32, (tk, tq), 0))
    contract_last = (((1,), (1,)), ((), ()))
    kk_ref[:, 0:LANES] = k_ref[0]
    kk_ref[:, LANES:] = kb_ref[0]
    vt_ref[...] = v_ref[0].astype(F32).T.astype(BF16)
    units = [(qb, e) for qb in range(seqlen // tq) for e in range(2)]

    def score_unit(u):
        qb, e = units[u]
        q0 = qb * tq
        q = q_ref[0, q0:q0 + tq, :] * jnp.asarray(scale, BF16)
        q_bias = qb_ref[0, q0:q0 + tq, :]
        first = FOX_BIAS_LANES * (2 * pair + e)
        mine = low if e == 0 else jnp.logical_not(low)
        own_bias = (lane >= first) & (lane < first + FOX_BIAS_LANES)
        qq = jnp.concatenate([jnp.where(mine, q, jnp.zeros_like(q)),
                              jnp.where(own_bias, q_bias, jnp.zeros_like(q_bias))], axis=1)
        scores = lambda r0, rows: lax.dot_general(
            kk_ref[r0:r0 + rows, :], qq, contract_last, preferred_element_type=F32)
        sc = s_ref.at[u % ATTN_SLOTS]
        if qb > 0:
            sc[0:q0, :] = scores(0, q0)
        sc[q0:q0 + tk, :] = jnp.where(keep, scores(q0, tk), -jnp.inf)

    def softmax_unit(u):
        qb, e = units[u]
        kv = qb * tq + tk
        sc, pr = s_ref.at[u % ATTN_SLOTS], p_ref.at[u % ATTN_SLOTS]
        m = sc[0:SUBLANES, :]
        for r in range(SUBLANES, kv, SUBLANES):
            m = jnp.maximum(m, sc[r:r + SUBLANES, :])
        m = jnp.max(m, axis=0, keepdims=True)
        l = jnp.zeros((SUBLANES, tq), F32)
        for r in range(0, kv, 2 * SUBLANES):
            p = jnp.exp(sc[r:r + 2 * SUBLANES, :] - m)
            pr[r:r + 2 * SUBLANES, :] = p.astype(BF16)
            l = l + p[0:SUBLANES] + p[SUBLANES:]
        l = jnp.sum(l, axis=0, keepdims=True)
        out_t = _dot(vt_ref[e * FOX_HEAD_DIM:(e + 1) * FOX_HEAD_DIM, 0:kv], pr[0:kv, :])
        return out_t, l

    score_unit(0)
    pending = []
    for u, (qb, e) in enumerate(units):
        if u + 1 < len(units):
            score_unit(u + 1)
        if e == 0 and pending:
            done_qb, parts = pending.pop()
            o_ref[0, done_qb * tq:(done_qb + 1) * tq, :] = jnp.concatenate(
                [out_t / l for out_t, l in parts], axis=0).T.astype(BF16)
        if e == 0:
            pending.append((qb, []))
        pending[-1][1].append(softmax_unit(u))
    done_qb, parts = pending.pop()
    o_ref[0, done_qb * tq:(done_qb + 1) * tq, :] = jnp.concatenate(
        [out_t / l for out_t, l in parts], axis=0).T.astype(BF16)


def _fox_attn(q, k, v, q_bias, k_bias):
    bsz, seqlen, _ = q.shape
    seq = pl.BlockSpec((1, seqlen, LANES), lambda b, j: (b, 0, j))
    shared = pl.BlockSpec((1, seqlen, LANES), lambda b, j: (b, 0, 0))
    return pl.pallas_call(
        _fox_attn_kernel, grid=(bsz, FOX_HEADS // 2),
        in_specs=[seq, seq, seq, shared, shared],
        out_specs=seq, out_shape=jax.ShapeDtypeStruct(q.shape, BF16),
        scratch_shapes=[pltpu.VMEM((seqlen, 2 * LANES), BF16),
                        pltpu.VMEM((LANES, seqlen), BF16),
                        pltpu.VMEM((ATTN_SLOTS, seqlen, ATTN_Q_BLOCK), F32),
                        pltpu.VMEM((ATTN_SLOTS, seqlen, ATTN_Q_BLOCK), BF16)],
        compiler_params=_params(2), name="fox_attn",
    )(q, k, v, q_bias, k_bias)


def _fox_out_kernel(x_ref, o_ref, wo_ref, y_ref):
    y_ref[...] = x_ref[...] + _dot(o_ref[...], wo_ref[...])


def _fox_out(x, o, wo):
    n = x.shape[0]
    row = pl.BlockSpec((ROW_BLOCK, D_MODEL), lambda i: (i, 0))
    return pl.pallas_call(
        _fox_out_kernel, grid=(n // ROW_BLOCK,),
        in_specs=[row, row, _resident((D_MODEL, D_MODEL))], out_specs=row,
        out_shape=jax.ShapeDtypeStruct((n, D_MODEL), F32),
        compiler_params=_params(1), name="fox_out",
    )(x, o, wo)


def _fox_mixer(x, bsz, seqlen, gain, w_in, b_f, wo):
    wq, wk, wv = (w_in[:, i * D_MODEL:(i + 1) * D_MODEL].astype(BF16) for i in range(3))
    wf = jnp.pad(w_in[:, 3 * D_MODEL:], ((0, 0), (0, LANES - FOX_HEADS))).astype(BF16)
    bf = jnp.pad(b_f, (0, LANES - FOX_HEADS)).reshape(1, LANES)
    q, k, v, q_bias, k_bias = _fox_proj(x, bsz, seqlen, gain, wq, wk, wv, wf, bf)
    shape, bias_shape = (bsz, seqlen, D_MODEL), (bsz, seqlen, LANES)
    o = _fox_attn(q.reshape(shape), k.reshape(shape), v.reshape(shape),
                  q_bias.reshape(bias_shape), k_bias.reshape(bias_shape))
    return _fox_out(x, o.reshape(bsz * seqlen, D_MODEL), wo.astype(BF16))


def kernel(x, p, norm_mix, norm_ffn, norm_ple, norm_final, ssm_lam_re, ssm_lam_im, ssm_log_dt,
           ssm_b_re, ssm_b_im, ssm_c_re, ssm_c_im, ssm_d, ssm_w_glu, conv_w_in, conv_w,
           conv_w_out, fox_w_in, fox_b_f, fox_w_out, mlp_w1, mlp_w2, ple_w, ple_gate_w):
    bsz, seqlen, _ = x.shape
    depth = p.shape[0]
    assert seqlen % ROW_BLOCK == 0 and seqlen % ATTN_Q_BLOCK == 0 and bsz % SUBLANES == 0
    n = bsz * seqlen
    row_vec = lambda v: v.reshape(1, D_MODEL)
    x = x.reshape(n, D_MODEL)
    for i in range(depth):
        kind, slot = i % N_MIXERS, i // N_MIXERS
        gain = row_vec(norm_mix[i])
        if kind == 0:
            weights = _s5_weights(ssm_lam_re[slot], ssm_lam_im[slot], ssm_log_dt[slot],
                                  ssm_b_re[slot], ssm_b_im[slot], ssm_c_re[slot], ssm_c_im[slot])
            w_glu = ssm_w_glu[slot].astype(BF16)
            x = _s5_mixer(x, bsz, seqlen, gain, row_vec(ssm_d[slot]), weights,
                          w_glu[:, :D_MODEL], w_glu[:, D_MODEL:])
        elif kind == 1:
            w_in = conv_w_in[slot].astype(BF16)
            x = _conv_mixer(x, bsz, seqlen, gain, w_in[:, :D_MODEL],
                            w_in[:, D_MODEL:2 * D_MODEL], w_in[:, 2 * D_MODEL:],
                            conv_w[slot], conv_w_out[slot].astype(BF16))
        else:
            x = _fox_mixer(x, bsz, seqlen, gain, fox_w_in[slot], fox_b_f[slot], fox_w_out[slot])
        x = _ffn_ple(x, p[i].reshape(n, PLE_DIM), row_vec(norm_ffn[i]), row_vec(norm_ple[i]),
                     row_vec(norm_final), mlp_w1[i].astype(BF16), mlp_w2[i].astype(BF16),
                     ple_gate_w[i].astype(BF16), ple_w[i].astype(BF16),
                     final_norm=(i == depth - 1))
    return x.reshape(bsz, seqlen, D_MODEL)
```

```python
import functools
import math

import jax
import jax.numpy as jnp
from jax import lax
from jax.experimental import pallas as pl
from jax.experimental.pallas import tpu as pltpu

D_MODEL = 1024
D_FF = 4 * D_MODEL
PLE_DIM = 256
NORM_EPS = 1e-6
N_MIXERS = 3
SSM_GROUP = 16
SSM_GROUPS = D_MODEL // SSM_GROUP
SSM_STATE = 64
CONV_WIDTH = 3
FOX_HEAD_DIM = 64
FOX_HEADS = D_MODEL // FOX_HEAD_DIM
FOX_BIAS_LANES = 6
LANES = 128
SUBLANES = 8
VMEM_LIMIT = 56 * 1024 * 1024

ROW_BLOCK = 512
FFN_ROW_BLOCK = 1024
FF_CHUNK = 512
ATTN_Q_BLOCK = 256
ATTN_LOOKAHEAD = 2
ATTN_SLOTS = ATTN_LOOKAHEAD + 2

SSM_STEP = 8
SSM_FOLD = SSM_STEP * LANES
SLABS = D_MODEL // LANES
SLAB_GROUPS = LANES // SSM_GROUP
SLAB_STATE = SLAB_GROUPS * SSM_STATE
SCAN_TIME_BLOCK = 1024
INTERLEAVE_TIME_BLOCK = 128
GLU_TIME_BLOCK = 128

BF16 = jnp.bfloat16
F32 = jnp.float32


def _rms(x, gain):
    return x * lax.rsqrt(jnp.mean(x * x, axis=-1, keepdims=True) + NORM_EPS) * gain


def _dot(a, b):
    return jnp.dot(a, b, preferred_element_type=F32)


def _params(n_axes):
    return pltpu.CompilerParams(dimension_semantics=("arbitrary",) * n_axes,
                                vmem_limit_bytes=VMEM_LIMIT)


def _resident(shape):
    nd = len(shape)
    return pl.BlockSpec(shape, lambda *_: (0,) * nd, pipeline_mode=pl.Buffered(1))


def _ffn_ple_kernel(x_ref, p_ref, gf_ref, gp_ref, gl_ref, w1_ref, w2_ref, wg_ref, wp_ref,
                    o_ref, h_ref, *, final_norm):
    h_ref[...] = _rms(x_ref[...], gf_ref[...]).astype(BF16)

    def mlp_chunk(col):
        a = jnp.maximum(_dot(h_ref[...], w1_ref[:, pl.ds(col, FF_CHUNK)]), 0.0)
        return _dot((a * a).astype(BF16), w2_ref[pl.ds(col, FF_CHUNK), :])

    o_ref[...] = x_ref[...] + mlp_chunk(0)

    def chunk(c, carry):
        o_ref[...] += mlp_chunk(pl.multiple_of(c * FF_CHUNK, FF_CHUNK))
        return carry

    lax.fori_loop(1, D_FF // FF_CHUNK, chunk, 0)
    x = o_ref[...]
    gate = jax.nn.sigmoid(_dot(_rms(x, gp_ref[...]).astype(BF16), wg_ref[...]))
    x = x + _dot(p_ref[...].astype(BF16), wp_ref[...]) * gate
    if final_norm:
        x = _rms(x, gl_ref[...])
    o_ref[...] = x


def _ffn_ple(x, p, g_ffn, g_ple, g_last, w1, w2, wg, wp, final_norm):
    n = x.shape[0]
    row = lambda w: pl.BlockSpec((FFN_ROW_BLOCK, w), lambda i: (i, 0))
    return pl.pallas_call(
        functools.partial(_ffn_ple_kernel, final_norm=final_norm),
        grid=(n // FFN_ROW_BLOCK,),
        in_specs=[row(D_MODEL), row(PLE_DIM),
                  _resident((1, D_MODEL)), _resident((1, D_MODEL)), _resident((1, D_MODEL)),
                  _resident((D_MODEL, D_FF)), _resident((D_FF, D_MODEL)),
                  _resident((D_MODEL, D_MODEL)), _resident((PLE_DIM, D_MODEL))],
        out_specs=row(D_MODEL),
        out_shape=jax.ShapeDtypeStruct((n, D_MODEL), F32),
        scratch_shapes=[pltpu.VMEM((FFN_ROW_BLOCK, D_MODEL), BF16)],
        compiler_params=_params(1),
        name="ffn_ple",
    )(x, p, g_ffn, g_ple, g_last, w1, w2, wg, wp)


def _s5_interleave_kernel(x_ref, g_ref, o_ref):
    for b in range(SUBLANES):
        o_ref[:, b, :] = _rms(x_ref[b], g_ref[...])


def _s5_interleave(x4, gain):
    groups, _, seqlen, _ = x4.shape
    tb = INTERLEAVE_TIME_BLOCK
    return pl.pallas_call(
        _s5_interleave_kernel, grid=(groups, seqlen // tb),
        in_specs=[pl.BlockSpec((None, SUBLANES, tb, D_MODEL), lambda g, t: (g, 0, t, 0)),
                  _resident((1, D_MODEL))],
        out_specs=pl.BlockSpec((None, tb, SUBLANES, D_MODEL), lambda g, t: (g, t, 0, 0)),
        out_shape=jax.ShapeDtypeStruct((groups, seqlen, SUBLANES, D_MODEL), F32),
        compiler_params=_params(2), name="s5_interleave",
    )(x4, gain)


def _s5_scan_kernel(u_ref, bs_ref, cs_ref, ds_ref, a_ref, y_ref, bu_ref, xs_ref, st_ref):
    steps = u_ref.shape[0]
    rows = steps * SUBLANES

    @pl.when(pl.program_id(2) == 0)
    def _():
        st_ref[...] = jnp.zeros_like(st_ref)

    u = jnp.concatenate([u_ref[:, i].reshape(rows, LANES).astype(BF16)
                         for i in range(SSM_STEP)], axis=1)
    bu_ref[...] = _dot(u, bs_ref[...])
    a_re = jnp.broadcast_to(a_ref[0:1, :], (SUBLANES, SLAB_STATE))
    a_im = jnp.broadcast_to(a_ref[1:2, :], (SUBLANES, SLAB_STATE))

    def step(k, carry):
        s_re, s_im = carry
        r0 = pl.multiple_of(k * SUBLANES, SUBLANES)
        xs_ref[pl.ds(r0, SUBLANES), 0:SLAB_STATE] = s_re
        xs_ref[pl.ds(r0, SUBLANES), SLAB_STATE:] = s_im
        b_re = bu_ref[pl.ds(r0, SUBLANES), 0:SLAB_STATE]
        b_im = bu_ref[pl.ds(r0, SUBLANES), SLAB_STATE:]
        return (a_re * s_re - a_im * s_im + b_re, a_re * s_im + a_im * s_re + b_im)

    s_re, s_im = lax.fori_loop(0, steps, step, (st_ref[0], st_ref[1]))
    st_ref[0] = s_re
    st_ref[1] = s_im
    y = _dot(xs_ref[...].astype(BF16), cs_ref[...]) + _dot(u, ds_ref[...])
    for j in range(SSM_STEP):
        y_ref[:, j] = y[:, j * LANES:(j + 1) * LANES].reshape(steps, SUBLANES, LANES)


def _s5_scan(u5, bs, cs, ds, a_pow):
    groups, total_steps = u5.shape[0], u5.shape[1]
    steps = min(SCAN_TIME_BLOCK // SSM_STEP, total_steps)
    rows = steps * SUBLANES
    act = pl.BlockSpec((None, steps, SSM_STEP, SUBLANES, LANES), lambda s, g, t: (g, t, 0, 0, s))
    per_slab = lambda *shape: pl.BlockSpec((None,) + shape, lambda s, g, t: (s,) + (0,) * len(shape))
    return pl.pallas_call(
        _s5_scan_kernel, grid=(SLABS, groups, total_steps // steps),
        in_specs=[act, per_slab(SSM_FOLD, 2 * SLAB_STATE), per_slab(2 * SLAB_STATE, SSM_FOLD),
                  per_slab(SSM_FOLD, SSM_FOLD), per_slab(2, SLAB_STATE)],
        out_specs=act, out_shape=jax.ShapeDtypeStruct(u5.shape, F32),
        scratch_shapes=[pltpu.VMEM((rows, 2 * SLAB_STATE), F32),
                        pltpu.VMEM((rows, 2 * SLAB_STATE), F32),
                        pltpu.VMEM((2, SUBLANES, SLAB_STATE), F32)],
        compiler_params=_params(3), name="s5_scan",
    )(u5, bs, cs, ds, a_pow)


def _s5_glu_kernel(x_ref, y_ref, g_ref, d_ref, wa_ref, wg_ref, o_ref, ybuf_ref):
    tb = x_ref.shape[1]
    for b in range(SUBLANES):
        ybuf_ref[b * tb:(b + 1) * tb, :] = y_ref[:, b, :]
    x = x_ref[...].reshape(SUBLANES * tb, D_MODEL)
    u = _rms(x, g_ref[...])
    z = jax.nn.gelu(ybuf_ref[...] + d_ref[...] * u).astype(BF16)
    out = x + _dot(z, wa_ref[...]) * jax.nn.sigmoid(_dot(z, wg_ref[...]))
    o_ref[...] = out.reshape(SUBLANES, tb, D_MODEL)


def _s5_glu(x4, y4, gain, d_skip, wa, wg):
    groups, _, seqlen, _ = x4.shape
    tb = GLU_TIME_BLOCK
    nat = pl.BlockSpec((None, SUBLANES, tb, D_MODEL), lambda g, t: (g, 0, t, 0))
    return pl.pallas_call(
        _s5_glu_kernel, grid=(groups, seqlen // tb),
        in_specs=[nat, pl.BlockSpec((None, tb, SUBLANES, D_MODEL), lambda g, t: (g, t, 0, 0)),
                  _resident((1, D_MODEL)), _resident((1, D_MODEL)),
                  _resident((D_MODEL, D_MODEL)), _resident((D_MODEL, D_MODEL))],
        out_specs=nat, out_shape=jax.ShapeDtypeStruct(x4.shape, F32),
        scratch_shapes=[pltpu.VMEM((SUBLANES * tb, D_MODEL), F32)],
        compiler_params=_params(2), name="s5_glu",
    )(x4, y4, gain, d_skip, wa, wg)


def _s5_weights(lam_re, lam_im, log_dt, b_re, b_im, c_re, c_im):
    hp = lax.Precision.HIGHEST
    S, G, H, P = SSM_STEP, SSM_GROUPS, SSM_GROUP, SSM_STATE
    dt = jnp.exp(log_dt)[:, None]
    def powers(n):
        n = n.astype(F32)[:, None, None]
        mag = jnp.exp(n * (lam_re * dt))
        return mag * jnp.cos(n * (lam_im * dt)), mag * jnp.sin(n * (lam_im * dt))

    pw_re, pw_im = powers(jnp.arange(S + 1))
    rev_re, rev_im = powers(S - 1 - jnp.arange(S))
    nr, ni = pw_re[1] - 1.0, pw_im[1]
    den = lam_re * lam_re + lam_im * lam_im
    coef_re = (nr * lam_re + ni * lam_im) / den
    coef_im = (ni * lam_re - nr * lam_im) / den
    bb_re = coef_re[..., None] * b_re - coef_im[..., None] * b_im
    bb_im = coef_re[..., None] * b_im + coef_im[..., None] * b_re
    ca_re = c_re[None] * pw_re[:, :, None, :] - c_im[None] * pw_im[:, :, None, :]
    ca_im = c_re[None] * pw_im[:, :, None, :] + c_im[None] * pw_re[:, :, None, :]
    bs_re = rev_re[:, :, :, None] * bb_re[None] - rev_im[:, :, :, None] * bb_im[None]
    bs_im = rev_re[:, :, :, None] * bb_im[None] + rev_im[:, :, :, None] * bb_re[None]
    kern = (jnp.einsum('tghp,gpk->tghk', ca_re[:S], bb_re, precision=hp)
            - jnp.einsum('tghp,gpk->tghk', ca_im[:S], bb_im, precision=hp))
    lag = jnp.arange(S)[None, :] - jnp.arange(S)[:, None]
    toep = jnp.where((lag >= 0)[:, :, None, None, None], kern[jnp.maximum(lag, 0)], 0.0)
    def block_diag(t):
        na, nb, _, nx, ny = t.shape
        t = t.reshape(na, nb, SLABS, SLAB_GROUPS, nx, ny).transpose(2, 1, 3, 5, 0, 4)
        compact = t.reshape(SLABS, nb * SLAB_GROUPS * ny, na * nx)
        col = jnp.arange(na * SLAB_GROUPS * nx)
        src = (col // (SLAB_GROUPS * nx)) * nx + col % nx
        widen = (jnp.arange(na * nx)[:, None] == src[None, :]).astype(F32)
        row_group = (jnp.arange(nb * SLAB_GROUPS * ny) // ny) % SLAB_GROUPS
        col_group = (col // nx) % SLAB_GROUPS
        wide = jnp.einsum('srk,kc->src', compact, widen, precision=hp)
        return jnp.where(row_group[:, None] == col_group[None, :], wide, 0.0).astype(BF16)

    bs = block_diag(jnp.stack([bs_re, bs_im], axis=0))
    cs = block_diag(jnp.stack([ca_re[1:], -ca_im[1:]], axis=1))
    ds = block_diag(toep.transpose(1, 0, 2, 3, 4))
    a_pow = jnp.stack([pw_re[S].reshape(SLABS, SLAB_STATE),
                       pw_im[S].reshape(SLABS, SLAB_STATE)], axis=1)
    return bs, cs, ds, a_pow


def _s5_mixer(x, bsz, seqlen, gain, d_skip, weights, wa, wg):
    groups = bsz // SUBLANES
    x4 = x.reshape(groups, SUBLANES, seqlen, D_MODEL)
    h = _s5_interleave(x4, gain)
    folded = (groups, seqlen // SSM_STEP, SSM_STEP, SUBLANES, D_MODEL)
    y = _s5_scan(h.reshape(folded), *weights)
    out = _s5_glu(x4, y.reshape(h.shape), gain, d_skip, wa, wg)
    return out.reshape(bsz * seqlen, D_MODEL)


def _conv_kernel(x_ref, g_ref, wb_ref, wc_ref, wv_ref, cw_ref, wo_ref, o_ref, z_ref):
    t = pl.program_id(1)
    rows = x_ref.shape[0]

    @pl.when(t == 0)
    def _():
        z_ref[0:SUBLANES, :] = jnp.zeros((SUBLANES, D_MODEL), F32)

    x = x_ref[...]
    h = _rms(x, g_ref[...]).astype(BF16)
    z_ref[SUBLANES:, :] = _dot(h, wc_ref[...]) * _dot(h, wv_ref[...])
    conv = cw_ref[CONV_WIDTH - 1:CONV_WIDTH, :] * z_ref[SUBLANES:, :]
    for tap in range(CONV_WIDTH - 1):
        back = CONV_WIDTH - 1 - tap
        conv = conv + cw_ref[tap:tap + 1, :] * z_ref[SUBLANES - back:SUBLANES - back + rows, :]
    z_ref[0:SUBLANES, :] = z_ref[rows:rows + SUBLANES, :]
    gated = (_dot(h, wb_ref[...]) * conv).astype(BF16)
    o_ref[...] = x + _dot(gated, wo_ref[...])


def _conv_mixer(x, bsz, seqlen, gain, wb, wc, wv, conv_w, wo):
    tblocks = seqlen // ROW_BLOCK
    row = pl.BlockSpec((ROW_BLOCK, D_MODEL), lambda b, t: (b * tblocks + t, 0))
    sq = _resident((D_MODEL, D_MODEL))
    return pl.pallas_call(
        _conv_kernel, grid=(bsz, tblocks),
        in_specs=[row, _resident((1, D_MODEL)), sq, sq, sq,
                  _resident((CONV_WIDTH, D_MODEL)), sq],
        out_specs=row, out_shape=jax.ShapeDtypeStruct(x.shape, F32),
        scratch_shapes=[pltpu.VMEM((ROW_BLOCK + SUBLANES, D_MODEL), F32)],
        compiler_params=_params(2), name="conv_mixer",
    )(x, gain, wb, wc, wv, conv_w, wo)


def _split3(x):
    hi = x.astype(BF16)
    rest = x - hi.astype(F32)
    mid = rest.astype(BF16)
    return hi, mid, (rest - mid.astype(F32)).astype(BF16)


def _fox_proj_kernel(x_ref, g_ref, wq_ref, wk_ref, wv_ref, wf_ref, bf_ref,
                     q_ref, k_ref, v_ref, qb_ref, kb_ref, carry_ref):
    t = pl.program_id(1)
    rows = x_ref.shape[0]

    @pl.when(t == 0)
    def _():
        carry_ref[...] = jnp.zeros_like(carry_ref)

    h = _rms(x_ref[...], g_ref[...]).astype(BF16)
    q_ref[...] = _dot(h, wq_ref[...]).astype(BF16)
    k_ref[...] = _dot(h, wk_ref[...]).astype(BF16)
    v_ref[...] = _dot(h, wv_ref[...]).astype(BF16)
    logit = _dot(h, wf_ref[...]) + bf_ref[...]
    log_f = jnp.minimum(logit, 0.0) - jnp.log1p(jnp.exp(-jnp.abs(logit)))
    tri = (lax.broadcasted_iota(jnp.int32, (rows, rows), 0)
           >= lax.broadcasted_iota(jnp.int32, (rows, rows), 1)).astype(BF16)
    cum = sum(_dot(tri, piece) for piece in _split3(log_f)) + carry_ref[0:1, :]
    carry_ref[0:1, :] = cum[rows - 1:rows, :]
    src = lax.broadcasted_iota(jnp.int32, (LANES, LANES), 0)
    dst = lax.broadcasted_iota(jnp.int32, (LANES, LANES), 1)
    place = lambda j: ((dst == FOX_BIAS_LANES * src + j) & (src < FOX_HEADS)).astype(BF16)
    pieces = _split3(cum)
    lane = lax.broadcasted_iota(jnp.int32, (1, LANES), 1)
    slot = lane % FOX_BIAS_LANES
    used = lane < FOX_BIAS_LANES * FOX_HEADS
    ones_q = (used & (slot >= 3)).astype(F32)
    ones_k = (used & (slot < 3)).astype(F32)
    qb_ref[...] = (sum(_dot(pieces[j], place(j)) for j in range(3)) + ones_q).astype(BF16)
    kb_ref[...] = (ones_k - sum(_dot(pieces[j], place(3 + j)) for j in range(3))).astype(BF16)


def _fox_proj(x, bsz, seqlen, gain, wq, wk, wv, wf, bf):
    tblocks = seqlen // ROW_BLOCK
    idx = lambda b, t: (b * tblocks + t, 0)
    row = pl.BlockSpec((ROW_BLOCK, D_MODEL), idx)
    bias = pl.BlockSpec((ROW_BLOCK, LANES), idx)
    sq = _resident((D_MODEL, D_MODEL))
    qkv = jax.ShapeDtypeStruct(x.shape, BF16)
    bias_shape = jax.ShapeDtypeStruct((x.shape[0], LANES), BF16)
    return pl.pallas_call(
        _fox_proj_kernel, grid=(bsz, tblocks),
        in_specs=[row, _resident((1, D_MODEL)), sq, sq, sq,
                  _resident((D_MODEL, LANES)), _resident((1, LANES))],
        out_specs=[row, row, row, bias, bias],
        out_shape=[qkv, qkv, qkv, bias_shape, bias_shape],
        scratch_shapes=[pltpu.VMEM((SUBLANES, LANES), F32)],
        compiler_params=_params(2), name="fox_proj",
    )(x, gain, wq, wk, wv, wf, bf)


def _fold_rows(x, op):
    acc = x[0:SUBLANES]
    for r in range(SUBLANES, x.shape[0], SUBLANES):
        acc = op(acc, x[r:r + SUBLANES])
    return acc


def _fox_attn_kernel(q_ref, k_ref, v_ref, qb_ref, kb_ref, o_ref, kk_ref, vt_ref, s_ref, p_ref):
    pair = pl.program_id(1)
    seqlen = q_ref.shape[1]
    tq = tk = ATTN_Q_BLOCK
    lane = lax.broadcasted_iota(jnp.int32, (1, LANES), 1)
    low = lane < FOX_HEAD_DIM
    scale = FOX_HEAD_DIM ** -0.5
    keep = (lax.broadcasted_iota(jnp.int32, (tk, tq), 1)
            >= lax.broadcasted_iota(jnp.int32, (tk, tq), 0))
    contract_last = (((1,), (1,)), ((), ()))
    kk_ref[:, 0:LANES] = k_ref[0]
    kk_ref[:, LANES:] = kb_ref[0]
    vt_ref[...] = v_ref[0].astype(F32).T.astype(BF16)
    units = [(qb, e) for qb in range(seqlen // tq) for e in range(2)]

    def score_unit(u):
        qb, e = units[u]
        q0 = qb * tq
        q = q_ref[0, q0:q0 + tq, :] * jnp.asarray(scale, BF16)
        q_bias = qb_ref[0, q0:q0 + tq, :]
        first = FOX_BIAS_LANES * (2 * pair + e)
        mine = low if e == 0 else jnp.logical_not(low)
        own_bias = (lane >= first) & (lane < first + FOX_BIAS_LANES)
        qq = jnp.concatenate([jnp.where(mine, q, jnp.zeros_like(q)),
                              jnp.where(own_bias, q_bias, jnp.zeros_like(q_bias))], axis=1)
        scores = lambda r0, rows: lax.dot_general(
            kk_ref[r0:r0 + rows, :], qq, contract_last, preferred_element_type=F32)
        sc = s_ref.at[u % ATTN_SLOTS]
        if qb > 0:
            sc[0:q0, :] = scores(0, q0)
        sc[q0:q0 + tk, :] = jnp.where(keep, scores(q0, tk), -jnp.inf)

    def softmax_unit(u):
        qb, e = units[u]
        kv = qb * tq + tk
        sc, pr = s_ref.at[u % ATTN_SLOTS], p_ref.at[u % ATTN_SLOTS]
        m = sc[0:SUBLANES, :]
        for r in range(SUBLANES, kv, SUBLANES):
            m = jnp.maximum(m, sc[r:r + SUBLANES, :])
        m = jnp.max(m, axis=0, keepdims=True)
        l = jnp.zeros((SUBLANES, tq), F32)
        for r in range(0, kv, 2 * SUBLANES):
            p = jnp.exp(sc[r:r + 2 * SUBLANES, :] - m)
            pr[r:r + 2 * SUBLANES, :] = p.astype(BF16)
            l = l + p[0:SUBLANES] + p[SUBLANES:]
        l = jnp.sum(l, axis=0, keepdims=True)
        out_t = _dot(vt_ref[e * FOX_HEAD_DIM:(e + 1) * FOX_HEAD_DIM, 0:kv], pr[0:kv, :])
        return out_t, l

    for u in range(ATTN_LOOKAHEAD):
        score_unit(u)
    pending = []
    for u, (qb, e) in enumerate(units):
        if u + ATTN_LOOKAHEAD < len(units):
            score_unit(u + ATTN_LOOKAHEAD)
        if e == 0 and pending:
            done_qb, parts = pending.pop()
            o_ref[0, done_qb * tq:(done_qb + 1) * tq, :] = jnp.concatenate(
                [out_t / l for out_t, l in parts], axis=0).T.astype(BF16)
        if e == 0:
            pending.append((qb, []))
        pending[-1][1].append(softmax_unit(u))
    done_qb, parts = pending.pop()
    o_ref[0, done_qb * tq:(done_qb + 1) * tq, :] = jnp.concatenate(
        [out_t / l for out_t, l in parts], axis=0).T.astype(BF16)


def _fox_attn(q, k, v, q_bias, k_bias):
    bsz, seqlen, _ = q.shape
    seq = pl.BlockSpec((1, seqlen, LANES), lambda b, j: (b, 0, j))
    shared = pl.BlockSpec((1, seqlen, LANES), lambda b, j: (b, 0, 0))
    return pl.pallas_call(
        _fox_attn_kernel, grid=(bsz, FOX_HEADS // 2),
        in_specs=[seq, seq, seq, shared, shared],
        out_specs=seq, out_shape=jax.ShapeDtypeStruct(q.shape, BF16),
        scratch_shapes=[pltpu.VMEM((seqlen, 2 * LANES), BF16),
                        pltpu.VMEM((LANES, seqlen), BF16),
                        pltpu.VMEM((ATTN_SLOTS, seqlen, ATTN_Q_BLOCK), F32),
                        pltpu.VMEM((ATTN_SLOTS, seqlen, ATTN_Q_BLOCK), BF16)],
        compiler_params=_params(2), name="fox_attn",
    )(q, k, v, q_bias, k_bias)


def _fox_out_kernel(x_ref, o_ref, wo_ref, y_ref):
    y_ref[...] = x_ref[...] + _dot(o_ref[...], wo_ref[...])


def _fox_out(x, o, wo):
    n = x.shape[0]
    row = pl.BlockSpec((ROW_BLOCK, D_MODEL), lambda i: (i, 0))
    return pl.pallas_call(
        _fox_out_kernel, grid=(n // ROW_BLOCK,),
        in_specs=[row, row, _resident((D_MODEL, D_MODEL))], out_specs=row,
        out_shape=jax.ShapeDtypeStruct((n, D_MODEL), F32),
        compiler_params=_params(1), name="fox_out",
    )(x, o, wo)


def _fox_mixer(x, bsz, seqlen, gain, w_in, b_f, wo):
    wq, wk, wv = (w_in[:, i * D_MODEL:(i + 1) * D_MODEL].astype(BF16) for i in range(3))
    wf = jnp.pad(w_in[:, 3 * D_MODEL:], ((0, 0), (0, LANES - FOX_HEADS))).astype(BF16)
    bf = jnp.pad(b_f, (0, LANES - FOX_HEADS)).reshape(1, LANES)
    q, k, v, q_bias, k_bias = _fox_proj(x, bsz, seqlen, gain, wq, wk, wv, wf, bf)
    shape, bias_shape = (bsz, seqlen, D_MODEL), (bsz, seqlen, LANES)
    o = _fox_attn(q.reshape(shape), k.reshape(shape), v.reshape(shape),
                  q_bias.reshape(bias_shape), k_bias.reshape(bias_shape))
    return _fox_out(x, o.reshape(bsz * seqlen, D_MODEL), wo.astype(BF16))


def kernel(x, p, norm_mix, norm_ffn, norm_ple, norm_final, ssm_lam_re, ssm_lam_im, ssm_log_dt,
           ssm_b_re, ssm_b_im, ssm_c_re, ssm_c_im, ssm_d, ssm_w_glu, conv_w_in, conv_w,
           conv_w_out, fox_w_in, fox_b_f, fox_w_out, mlp_w1, mlp_w2, ple_w, ple_gate_w):
    bsz, seqlen, _ = x.shape
    depth = p.shape[0]
    assert seqlen % ROW_BLOCK == 0 and seqlen % ATTN_Q_BLOCK == 0 and bsz % SUBLANES == 0
    n = bsz * seqlen
    row_vec = lambda v: v.reshape(1, D_MODEL)
    x = x.reshape(n, D_MODEL)
    for i in range(depth):
        kind, slot = i % N_MIXERS, i // N_MIXERS
        gain = row_vec(norm_mix[i])
        if kind == 0:
            weights = _s5_weights(ssm_lam_re[slot], ssm_lam_im[slot], ssm_log_dt[slot],
                                  ssm_b_re[slot], ssm_b_im[slot], ssm_c_re[slot], ssm_c_im[slot])
            w_glu = ssm_w_glu[slot].astype(BF16)
            x = _s5_mixer(x, bsz, seqlen, gain, row_vec(ssm_d[slot]), weights,
                          w_glu[:, :D_MODEL], w_glu[:, D_MODEL:])
        elif kind == 1:
            w_in = conv_w_in[slot].astype(BF16)
            x = _conv_mixer(x, bsz, seqlen, gain, w_in[:, :D_MODEL],
                            w_in[:, D_MODEL:2 * D_MODEL], w_in[:, 2 * D_MODEL:],
                            conv_w[slot], conv_w_out[slot].astype(BF16))
        else:
            x = _fox_mixer(x, bsz, seqlen, gain, fox_w_in[slot], fox_b_f[slot], fox_w_out[slot])
        x = _ffn_ple(x, p[i].reshape(n, PLE_DIM), row_vec(norm_ffn[i]), row_vec(norm_ple[i]),
                     row_vec(norm_final), mlp_w1[i].astype(BF16), mlp_w2[i].astype(BF16),
                     ple_gate_w[i].astype(BF16), ple_w[i].astype(BF16),
                     final_norm=(i == depth - 1))
    return x.reshape(bsz, seqlen, D_MODEL)
```

```python
import functools
import math

import jax
import jax.numpy as jnp
from jax import lax
from jax.experimental import pallas as pl
from jax.experimental.pallas import tpu as pltpu

D_MODEL = 1024
D_FF = 4 * D_MODEL
PLE_DIM = 256
NORM_EPS = 1e-6
N_MIXERS = 3
SSM_GROUP = 16
SSM_GROUPS = D_MODEL // SSM_GROUP
SSM_STATE = 64
CONV_WIDTH = 3
FOX_HEAD_DIM = 64
FOX_HEADS = D_MODEL // FOX_HEAD_DIM
FOX_PAIRS = FOX_HEADS // 2
FOX_BIAS_LANES = 6
LANES = 128
SUBLANES = 8
VMEM_LIMIT = 56 * 1024 * 1024

ROW_BLOCK = 512
FFN_ROW_BLOCK = 1024
FF_CHUNK = 1024
ATTN_Q_BLOCK = 256
ATTN_LOOKAHEAD = 2
ATTN_SLOTS = ATTN_LOOKAHEAD + 2

SSM_STEP = 8
SSM_FOLD = SSM_STEP * LANES
SLABS = D_MODEL // LANES
SLAB_GROUPS = LANES // SSM_GROUP
SLAB_STATE = SLAB_GROUPS * SSM_STATE
SCAN_TIME_BLOCK = 1024
INTERLEAVE_TIME_BLOCK = 128
GLU_TIME_BLOCK = 128

BF16 = jnp.bfloat16
F32 = jnp.float32


def _rms(x, gain):
    return x * lax.rsqrt(jnp.mean(x * x, axis=-1, keepdims=True) + NORM_EPS) * gain


def _dot(a, b):
    return jnp.dot(a, b, preferred_element_type=F32)


def _params(n_axes):
    return pltpu.CompilerParams(dimension_semantics=("arbitrary",) * n_axes,
                                vmem_limit_bytes=VMEM_LIMIT)


def _resident(shape):
    nd = len(shape)
    return pl.BlockSpec(shape, lambda *_: (0,) * nd, pipeline_mode=pl.Buffered(1))


def _ffn_ple_kernel(x_ref, p_ref, gf_ref, gp_ref, gl_ref, w1_ref, w2_ref, wg_ref, wp_ref,
                    o_ref, h_ref, *, final_norm):
    h_ref[...] = _rms(x_ref[...], gf_ref[...]).astype(BF16)

    def mlp_chunk(col):
        a = jnp.maximum(_dot(h_ref[...], w1_ref[:, pl.ds(col, FF_CHUNK)]), 0.0)
        return _dot((a * a).astype(BF16), w2_ref[pl.ds(col, FF_CHUNK), :])

    o_ref[...] = x_ref[...] + mlp_chunk(0)

    def chunk(c, carry):
        o_ref[...] += mlp_chunk(pl.multiple_of(c * FF_CHUNK, FF_CHUNK))
        return carry

    lax.fori_loop(1, D_FF // FF_CHUNK, chunk, 0)
    x = o_ref[...]
    gate = jax.nn.sigmoid(_dot(_rms(x, gp_ref[...]).astype(BF16), wg_ref[...]))
    x = x + _dot(p_ref[...].astype(BF16), wp_ref[...]) * gate
    if final_norm:
        x = _rms(x, gl_ref[...])
    o_ref[...] = x


def _ffn_ple(x, p, layer, g_ffn, g_ple, g_last, w1, w2, wg, wp, final_norm):
    n = x.shape[0]
    row = lambda w: pl.BlockSpec((FFN_ROW_BLOCK, w), lambda i: (i, 0))
    return pl.pallas_call(
        functools.partial(_ffn_ple_kernel, final_norm=final_norm),
        grid=(n // FFN_ROW_BLOCK,),
        in_specs=[row(D_MODEL),
                  pl.BlockSpec((None, FFN_ROW_BLOCK, PLE_DIM), lambda i: (layer, i, 0)),
                  _resident((1, D_MODEL)), _resident((1, D_MODEL)), _resident((1, D_MODEL)),
                  _resident((D_MODEL, D_FF)), _resident((D_FF, D_MODEL)),
                  _resident((D_MODEL, D_MODEL)), _resident((PLE_DIM, D_MODEL))],
        out_specs=row(D_MODEL),
        out_shape=jax.ShapeDtypeStruct((n, D_MODEL), F32),
        scratch_shapes=[pltpu.VMEM((FFN_ROW_BLOCK, D_MODEL), BF16)],
        compiler_params=_params(1),
        name="ffn_ple",
    )(x, p, g_ffn, g_ple, g_last, w1, w2, wg, wp)


def _s5_interleave_kernel(x_ref, g_ref, o_ref):
    for b in range(SUBLANES):
        o_ref[:, b, :] = _rms(x_ref[b], g_ref[...])


def _s5_interleave(x4, gain):
    groups, _, seqlen, _ = x4.shape
    tb = INTERLEAVE_TIME_BLOCK
    return pl.pallas_call(
        _s5_interleave_kernel, grid=(groups, seqlen // tb),
        in_specs=[pl.BlockSpec((None, SUBLANES, tb, D_MODEL), lambda g, t: (g, 0, t, 0)),
                  _resident((1, D_MODEL))],
        out_specs=pl.BlockSpec((None, tb, SUBLANES, D_MODEL), lambda g, t: (g, t, 0, 0)),
        out_shape=jax.ShapeDtypeStruct((groups, seqlen, SUBLANES, D_MODEL), F32),
        compiler_params=_params(2), name="s5_interleave",
    )(x4, gain)


def _s5_scan_kernel(u_ref, bs_ref, cs_ref, ds_ref, a_ref, y_ref, bu_ref, xs_ref, st_ref):
    steps = u_ref.shape[0]
    rows = steps * SUBLANES

    @pl.when(pl.program_id(2) == 0)
    def _():
        st_ref[...] = jnp.zeros_like(st_ref)

    u = jnp.concatenate([u_ref[:, i].reshape(rows, LANES).astype(BF16)
                         for i in range(SSM_STEP)], axis=1)
    bu_ref[...] = _dot(u, bs_ref[...])
    a_re = jnp.broadcast_to(a_ref[0:1, :], (SUBLANES, SLAB_STATE))
    a_im = jnp.broadcast_to(a_ref[1:2, :], (SUBLANES, SLAB_STATE))

    def step(k, carry):
        s_re, s_im = carry
        r0 = pl.multiple_of(k * SUBLANES, SUBLANES)
        xs_ref[pl.ds(r0, SUBLANES), 0:SLAB_STATE] = s_re
        xs_ref[pl.ds(r0, SUBLANES), SLAB_STATE:] = s_im
        b_re = bu_ref[pl.ds(r0, SUBLANES), 0:SLAB_STATE]
        b_im = bu_ref[pl.ds(r0, SUBLANES), SLAB_STATE:]
        return (a_re * s_re - a_im * s_im + b_re, a_re * s_im + a_im * s_re + b_im)

    s_re, s_im = lax.fori_loop(0, steps, step, (st_ref[0], st_ref[1]))
    st_ref[0] = s_re
    st_ref[1] = s_im
    y = _dot(xs_ref[...].astype(BF16), cs_ref[...]) + _dot(u, ds_ref[...])
    for j in range(SSM_STEP):
        y_ref[:, j] = y[:, j * LANES:(j + 1) * LANES].reshape(steps, SUBLANES, LANES)


def _s5_scan(u5, bs, cs, ds, a_pow):
    groups, total_steps = u5.shape[0], u5.shape[1]
    steps = min(SCAN_TIME_BLOCK // SSM_STEP, total_steps)
    rows = steps * SUBLANES
    act = pl.BlockSpec((None, steps, SSM_STEP, SUBLANES, LANES), lambda s, g, t: (g, t, 0, 0, s))
    per_slab = lambda *shape: pl.BlockSpec((None,) + shape, lambda s, g, t: (s,) + (0,) * len(shape))
    return pl.pallas_call(
        _s5_scan_kernel, grid=(SLABS, groups, total_steps // steps),
        in_specs=[act, per_slab(SSM_FOLD, 2 * SLAB_STATE), per_slab(2 * SLAB_STATE, SSM_FOLD),
                  per_slab(SSM_FOLD, SSM_FOLD), per_slab(2, SLAB_STATE)],
        out_specs=act, out_shape=jax.ShapeDtypeStruct(u5.shape, F32),
        scratch_shapes=[pltpu.VMEM((rows, 2 * SLAB_STATE), F32),
                        pltpu.VMEM((rows, 2 * SLAB_STATE), F32),
                        pltpu.VMEM((2, SUBLANES, SLAB_STATE), F32)],
        compiler_params=_params(3), name="s5_scan",
    )(u5, bs, cs, ds, a_pow)


def _s5_glu_kernel(x_ref, y_ref, g_ref, d_ref, wa_ref, wg_ref, o_ref, ybuf_ref):
    tb = x_ref.shape[1]
    for b in range(SUBLANES):
        ybuf_ref[b * tb:(b + 1) * tb, :] = y_ref[:, b, :]
    x = x_ref[...].reshape(SUBLANES * tb, D_MODEL)
    u = _rms(x, g_ref[...])
    z = jax.nn.gelu(ybuf_ref[...] + d_ref[...] * u).astype(BF16)
    out = x + _dot(z, wa_ref[...]) * jax.nn.sigmoid(_dot(z, wg_ref[...]))
    o_ref[...] = out.reshape(SUBLANES, tb, D_MODEL)


def _s5_glu(x4, y4, gain, d_skip, wa, wg):
    groups, _, seqlen, _ = x4.shape
    tb = GLU_TIME_BLOCK
    nat = pl.BlockSpec((None, SUBLANES, tb, D_MODEL), lambda g, t: (g, 0, t, 0))
    return pl.pallas_call(
        _s5_glu_kernel, grid=(groups, seqlen // tb),
        in_specs=[nat, pl.BlockSpec((None, tb, SUBLANES, D_MODEL), lambda g, t: (g, t, 0, 0)),
                  _resident((1, D_MODEL)), _resident((1, D_MODEL)),
                  _resident((D_MODEL, D_MODEL)), _resident((D_MODEL, D_MODEL))],
        out_specs=nat, out_shape=jax.ShapeDtypeStruct(x4.shape, F32),
        scratch_shapes=[pltpu.VMEM((SUBLANES * tb, D_MODEL), F32)],
        compiler_params=_params(2), name="s5_glu",
    )(x4, y4, gain, d_skip, wa, wg)


def _s5_weights(lam_re, lam_im, log_dt, b_re, b_im, c_re, c_im):
    hp = lax.Precision.HIGHEST
    S, G, H, P = SSM_STEP, SSM_GROUPS, SSM_GROUP, SSM_STATE
    dt = jnp.exp(log_dt)[:, None]
    def powers(n):
        n = n.astype(F32)[:, None, None]
        mag = jnp.exp(n * (lam_re * dt))
        return mag * jnp.cos(n * (lam_im * dt)), mag * jnp.sin(n * (lam_im * dt))

    pw_re, pw_im = powers(jnp.arange(S + 1))
    rev_re, rev_im = powers(S - 1 - jnp.arange(S))
    nr, ni = pw_re[1] - 1.0, pw_im[1]
    den = lam_re * lam_re + lam_im * lam_im
    coef_re = (nr * lam_re + ni * lam_im) / den
    coef_im = (ni * lam_re - nr * lam_im) / den
    bb_re = coef_re[..., None] * b_re - coef_im[..., None] * b_im
    bb_im = coef_re[..., None] * b_im + coef_im[..., None] * b_re
    ca_re = c_re[None] * pw_re[:, :, None, :] - c_im[None] * pw_im[:, :, None, :]
    ca_im = c_re[None] * pw_im[:, :, None, :] + c_im[None] * pw_re[:, :, None, :]
    bs_re = rev_re[:, :, :, None] * bb_re[None] - rev_im[:, :, :, None] * bb_im[None]
    bs_im = rev_re[:, :, :, None] * bb_im[None] + rev_im[:, :, :, None] * bb_re[None]
    kern = (jnp.einsum('tghp,gpk->tghk', ca_re[:S], bb_re, precision=hp)
            - jnp.einsum('tghp,gpk->tghk', ca_im[:S], bb_im, precision=hp))
    lag = jnp.arange(S)[None, :] - jnp.arange(S)[:, None]
    toep = jnp.where((lag >= 0)[:, :, None, None, None], kern[jnp.maximum(lag, 0)], 0.0)
    def block_diag(t):
        na, nb, _, nx, ny = t.shape
        t = t.reshape(na, nb, SLABS, SLAB_GROUPS, nx, ny).transpose(2, 1, 3, 5, 0, 4)
        compact = t.reshape(SLABS, nb * SLAB_GROUPS * ny, na * nx)
        col = jnp.arange(na * SLAB_GROUPS * nx)
        src = (col // (SLAB_GROUPS * nx)) * nx + col % nx
        widen = (jnp.arange(na * nx)[:, None] == src[None, :]).astype(F32)
        row_group = (jnp.arange(nb * SLAB_GROUPS * ny) // ny) % SLAB_GROUPS
        col_group = (col // nx) % SLAB_GROUPS
        wide = jnp.einsum('srk,kc->src', compact, widen, precision=hp)
        return jnp.where(row_group[:, None] == col_group[None, :], wide, 0.0).astype(BF16)

    bs = block_diag(jnp.stack([bs_re, bs_im], axis=0))
    cs = block_diag(jnp.stack([ca_re[1:], -ca_im[1:]], axis=1))
    ds = block_diag(toep.transpose(1, 0, 2, 3, 4))
    a_pow = jnp.stack([pw_re[S].reshape(SLABS, SLAB_STATE),
                       pw_im[S].reshape(SLABS, SLAB_STATE)], axis=1)
    return bs, cs, ds, a_pow


def _s5_mixer(x, bsz, seqlen, gain, d_skip, weights, wa, wg):
    groups = bsz // SUBLANES
    x4 = x.reshape(groups, SUBLANES, seqlen, D_MODEL)
    h = _s5_interleave(x4, gain)
    folded = (groups, seqlen // SSM_STEP, SSM_STEP, SUBLANES, D_MODEL)
    y = _s5_scan(h.reshape(folded), *weights)
    out = _s5_glu(x4, y.reshape(h.shape), gain, d_skip, wa, wg)
    return out.reshape(bsz * seqlen, D_MODEL)


def _conv_kernel(x_ref, g_ref, wb_ref, wc_ref, wv_ref, cw_ref, wo_ref, o_ref, z_ref):
    t = pl.program_id(1)
    rows = x_ref.shape[0]

    @pl.when(t == 0)
    def _():
        z_ref[0:SUBLANES, :] = jnp.zeros((SUBLANES, D_MODEL), F32)

    x = x_ref[...]
    h = _rms(x, g_ref[...]).astype(BF16)
    z_ref[SUBLANES:, :] = _dot(h, wc_ref[...]) * _dot(h, wv_ref[...])
    conv = cw_ref[CONV_WIDTH - 1:CONV_WIDTH, :] * z_ref[SUBLANES:, :]
    for tap in range(CONV_WIDTH - 1):
        back = CONV_WIDTH - 1 - tap
        conv = conv + cw_ref[tap:tap + 1, :] * z_ref[SUBLANES - back:SUBLANES - back + rows, :]
    z_ref[0:SUBLANES, :] = z_ref[rows:rows + SUBLANES, :]
    gated = (_dot(h, wb_ref[...]) * conv).astype(BF16)
    o_ref[...] = x + _dot(gated, wo_ref[...])


def _conv_mixer(x, bsz, seqlen, gain, wb, wc, wv, conv_w, wo):
    tblocks = seqlen // ROW_BLOCK
    row = pl.BlockSpec((ROW_BLOCK, D_MODEL), lambda b, t: (b * tblocks + t, 0))
    sq = _resident((D_MODEL, D_MODEL))
    return pl.pallas_call(
        _conv_kernel, grid=(bsz, tblocks),
        in_specs=[row, _resident((1, D_MODEL)), sq, sq, sq,
                  _resident((CONV_WIDTH, D_MODEL)), sq],
        out_specs=row, out_shape=jax.ShapeDtypeStruct(x.shape, F32),
        scratch_shapes=[pltpu.VMEM((ROW_BLOCK + SUBLANES, D_MODEL), F32)],
        compiler_params=_params(2), name="conv_mixer",
    )(x, gain, wb, wc, wv, conv_w, wo)


def _split3(x):
    hi = x.astype(BF16)
    rest = x - hi.astype(F32)
    mid = rest.astype(BF16)
    return hi, mid, (rest - mid.astype(F32)).astype(BF16)


def _fox_proj_kernel(x_ref, g_ref, wq_ref, wk_ref, wv_ref, wf_ref, bf_ref,
                     q_ref, k_ref, v_ref, qb_ref, kb_ref, carry_ref):
    t = pl.program_id(1)
    rows = x_ref.shape[0]

    @pl.when(t == 0)
    def _():
        carry_ref[...] = jnp.zeros_like(carry_ref)

    h = _rms(x_ref[...], g_ref[...]).astype(BF16)
    for w_ref, out_ref in ((wq_ref, q_ref), (wk_ref, k_ref), (wv_ref, v_ref)):
        proj = _dot(h, w_ref[...]).astype(BF16)
        for j in range(FOX_PAIRS):
            out_ref[j] = proj[:, j * LANES:(j + 1) * LANES]
    logit = _dot(h, wf_ref[...]) + bf_ref[...]
    log_f = jnp.minimum(logit, 0.0) - jnp.log1p(jnp.exp(-jnp.abs(logit)))
    tri = (lax.broadcasted_iota(jnp.int32, (rows, rows), 0)
           >= lax.broadcasted_iota(jnp.int32, (rows, rows), 1)).astype(BF16)
    cum = sum(_dot(tri, piece) for piece in _split3(log_f)) + carry_ref[0:1, :]
    carry_ref[0:1, :] = cum[rows - 1:rows, :]
    src = lax.broadcasted_iota(jnp.int32, (LANES, LANES), 0)
    dst = lax.broadcasted_iota(jnp.int32, (LANES, LANES), 1)
    place = lambda j: ((dst == FOX_BIAS_LANES * src + j) & (src < FOX_HEADS)).astype(BF16)
    pieces = _split3(cum)
    lane = lax.broadcasted_iota(jnp.int32, (1, LANES), 1)
    slot = lane % FOX_BIAS_LANES
    used = lane < FOX_BIAS_LANES * FOX_HEADS
    ones_q = (used & (slot >= 3)).astype(F32)
    ones_k = (used & (slot < 3)).astype(F32)
    qb_ref[...] = (sum(_dot(pieces[j], place(j)) for j in range(3)) + ones_q).astype(BF16)
    kb_ref[...] = (ones_k - sum(_dot(pieces[j], place(3 + j)) for j in range(3))).astype(BF16)


def _fox_proj(x, bsz, seqlen, gain, wq, wk, wv, wf, bf):
    tblocks = seqlen // ROW_BLOCK
    idx = lambda b, t: (b * tblocks + t, 0)
    row = pl.BlockSpec((ROW_BLOCK, D_MODEL), idx)
    bias = pl.BlockSpec((ROW_BLOCK, LANES), idx)
    pairs = pl.BlockSpec((None, FOX_PAIRS, ROW_BLOCK, LANES), lambda b, t: (b, 0, t, 0))
    sq = _resident((D_MODEL, D_MODEL))
    qkv = jax.ShapeDtypeStruct((bsz, FOX_PAIRS, seqlen, LANES), BF16)
    bias_shape = jax.ShapeDtypeStruct((x.shape[0], LANES), BF16)
    return pl.pallas_call(
        _fox_proj_kernel, grid=(bsz, tblocks),
        in_specs=[row, _resident((1, D_MODEL)), sq, sq, sq,
                  _resident((D_MODEL, LANES)), _resident((1, LANES))],
        out_specs=[pairs, pairs, pairs, bias, bias],
        out_shape=[qkv, qkv, qkv, bias_shape, bias_shape],
        scratch_shapes=[pltpu.VMEM((SUBLANES, LANES), F32)],
        compiler_params=_params(2), name="fox_proj",
    )(x, gain, wq, wk, wv, wf, bf)


def _fold_rows(x, op):
    acc = x[0:SUBLANES]
    for r in range(SUBLANES, x.shape[0], SUBLANES):
        acc = op(acc, x[r:r + SUBLANES])
    return acc


def _fox_attn_kernel(q_ref, k_ref, v_ref, qb_ref, kb_ref, o_ref, kk_ref, vt_ref, s_ref, p_ref):
    pair = pl.program_id(1)
    seqlen = q_ref.shape[1]
    tq = tk = ATTN_Q_BLOCK
    lane = lax.broadcasted_iota(jnp.int32, (1, LANES), 1)
    low = lane < FOX_HEAD_DIM
    scale = FOX_HEAD_DIM ** -0.5
    keep = (lax.broadcasted_iota(jnp.int32, (tk, tq), 1)
            >= lax.broadcasted_iota(jnp.int32, (tk, tq), 0))
    contract_last = (((1,), (1,)), ((), ()))
    kk_ref[:, 0:LANES] = k_ref[0]
    kk_ref[:, LANES:] = kb_ref[0]
    vt_ref[...] = v_ref[0].astype(F32).T.astype(BF16)
    units = [(qb, e) for qb in range(seqlen // tq) for e in range(2)]

    def score_unit(u):
        qb, e = units[u]
        q0 = qb * tq
        q = q_ref[0, q0:q0 + tq, :] * jnp.asarray(scale, BF16)
        q_bias = qb_ref[0, q0:q0 + tq, :]
        first = FOX_BIAS_LANES * (2 * pair + e)
        mine = low if e == 0 else jnp.logical_not(low)
        own_bias = (lane >= first) & (lane < first + FOX_BIAS_LANES)
        qq = jnp.concatenate([jnp.where(mine, q, jnp.zeros_like(q)),
                              jnp.where(own_bias, q_bias, jnp.zeros_like(q_bias))], axis=1)
        scores = lambda r0, rows: lax.dot_general(
            kk_ref[r0:r0 + rows, :], qq, contract_last, preferred_element_type=F32)
        sc = s_ref.at[u % ATTN_SLOTS]
        if qb > 0:
            sc[0:q0, :] = scores(0, q0)
        sc[q0:q0 + tk, :] = jnp.where(keep, scores(q0, tk), -jnp.inf)

    def softmax_unit(u):
        qb, e = units[u]
        kv = qb * tq + tk
        sc, pr = s_ref.at[u % ATTN_SLOTS], p_ref.at[u % ATTN_SLOTS]
        m = sc[0:SUBLANES, :]
        for r in range(SUBLANES, kv, SUBLANES):
            m = jnp.maximum(m, sc[r:r + SUBLANES, :])
        m = jnp.max(m, axis=0, keepdims=True)
        l = jnp.zeros((SUBLANES, tq), F32)
        for r in range(0, kv, 2 * SUBLANES):
            p = jnp.exp(sc[r:r + 2 * SUBLANES, :] - m)
            pr[r:r + 2 * SUBLANES, :] = p.astype(BF16)
            l = l + p[0:SUBLANES] + p[SUBLANES:]
        l = jnp.sum(l, axis=0, keepdims=True)
        out_t = _dot(vt_ref[e * FOX_HEAD_DIM:(e + 1) * FOX_HEAD_DIM, 0:kv], pr[0:kv, :])
        return out_t, l

    for u in range(ATTN_LOOKAHEAD):
        score_unit(u)
    pending = []
    for u, (qb, e) in enumerate(units):
        if u + ATTN_LOOKAHEAD < len(units):
            score_unit(u + ATTN_LOOKAHEAD)
        if e == 0 and pending:
            done_qb, parts = pending.pop()
            o_ref[0, done_qb * tq:(done_qb + 1) * tq, :] = jnp.concatenate(
                [out_t / l for out_t, l in parts], axis=0).T.astype(BF16)
        if e == 0:
            pending.append((qb, []))
        pending[-1][1].append(softmax_unit(u))
    done_qb, parts = pending.pop()
    o_ref[0, done_qb * tq:(done_qb + 1) * tq, :] = jnp.concatenate(
        [out_t / l for out_t, l in parts], axis=0).T.astype(BF16)


def _fox_attn(q, k, v, q_bias, k_bias):
    bsz, _, seqlen, _ = q.shape
    seq = pl.BlockSpec((None, 1, seqlen, LANES), lambda b, j: (b, j, 0, 0))
    shared = pl.BlockSpec((1, seqlen, LANES), lambda b, j: (b, 0, 0))
    return pl.pallas_call(
        _fox_attn_kernel, grid=(bsz, FOX_PAIRS),
        in_specs=[seq, seq, seq, shared, shared],
        out_specs=seq, out_shape=jax.ShapeDtypeStruct(q.shape, BF16),
        scratch_shapes=[pltpu.VMEM((seqlen, 2 * LANES), BF16),
                        pltpu.VMEM((LANES, seqlen), BF16),
                        pltpu.VMEM((ATTN_SLOTS, seqlen, ATTN_Q_BLOCK), F32),
                        pltpu.VMEM((ATTN_SLOTS, seqlen, ATTN_Q_BLOCK), BF16)],
        compiler_params=_params(2), name="fox_attn",
    )(q, k, v, q_bias, k_bias)


def _fox_out_kernel(x_ref, o_ref, wo_ref, y_ref):
    o = jnp.concatenate([o_ref[j] for j in range(FOX_PAIRS)], axis=1)
    y_ref[...] = x_ref[...] + _dot(o, wo_ref[...])


def _fox_out(x, o, wo):
    bsz, _, seqlen, _ = o.shape
    tblocks = seqlen // ROW_BLOCK
    row = pl.BlockSpec((ROW_BLOCK, D_MODEL), lambda b, t: (b * tblocks + t, 0))
    pairs = pl.BlockSpec((None, FOX_PAIRS, ROW_BLOCK, LANES), lambda b, t: (b, 0, t, 0))
    return pl.pallas_call(
        _fox_out_kernel, grid=(bsz, tblocks),
        in_specs=[row, pairs, _resident((D_MODEL, D_MODEL))], out_specs=row,
        out_shape=jax.ShapeDtypeStruct(x.shape, F32),
        compiler_params=_params(2), name="fox_out",
    )(x, o, wo)


def _fox_mixer(x, bsz, seqlen, gain, w_in, b_f, wo):
    wq, wk, wv = (w_in[:, i * D_MODEL:(i + 1) * D_MODEL].astype(BF16) for i in range(3))
    wf = jnp.pad(w_in[:, 3 * D_MODEL:], ((0, 0), (0, LANES - FOX_HEADS))).astype(BF16)
    bf = jnp.pad(b_f, (0, LANES - FOX_HEADS)).reshape(1, LANES)
    q, k, v, q_bias, k_bias = _fox_proj(x, bsz, seqlen, gain, wq, wk, wv, wf, bf)
    bias_shape = (bsz, seqlen, LANES)
    o = _fox_attn(q, k, v, q_bias.reshape(bias_shape), k_bias.reshape(bias_shape))
    return _fox_out(x, o, wo.astype(BF16))


def kernel(x, p, norm_mix, norm_ffn, norm_ple, norm_final, ssm_lam_re, ssm_lam_im, ssm_log_dt,
           ssm_b_re, ssm_b_im, ssm_c_re, ssm_c_im, ssm_d, ssm_w_glu, conv_w_in, conv_w,
           conv_w_out, fox_w_in, fox_b_f, fox_w_out, mlp_w1, mlp_w2, ple_w, ple_gate_w):
    bsz, seqlen, _ = x.shape
    depth = p.shape[0]
    assert seqlen % ROW_BLOCK == 0 and seqlen % ATTN_Q_BLOCK == 0 and bsz % SUBLANES == 0
    n = bsz * seqlen
    row_vec = lambda v: v.reshape(1, D_MODEL)
    x = x.reshape(n, D_MODEL)
    for i in range(depth):
        kind, slot = i % N_MIXERS, i // N_MIXERS
        gain = row_vec(norm_mix[i])
        if kind == 0:
            weights = _s5_weights(ssm_lam_re[slot], ssm_lam_im[slot], ssm_log_dt[slot],
                                  ssm_b_re[slot], ssm_b_im[slot], ssm_c_re[slot], ssm_c_im[slot])
            w_glu = ssm_w_glu[slot].astype(BF16)
            x = _s5_mixer(x, bsz, seqlen, gain, row_vec(ssm_d[slot]), weights,
                          w_glu[:, :D_MODEL], w_glu[:, D_MODEL:])
        elif kind == 1:
            w_in = conv_w_in[slot].astype(BF16)
            x = _conv_mixer(x, bsz, seqlen, gain, w_in[:, :D_MODEL],
                            w_in[:, D_MODEL:2 * D_MODEL], w_in[:, 2 * D_MODEL:],
                            conv_w[slot], conv_w_out[slot].astype(BF16))
        else:
            x = _fox_mixer(x, bsz, seqlen, gain, fox_w_in[slot], fox_b_f[slot], fox_w_out[slot])
        x = _ffn_ple(x, p.reshape(depth, n, PLE_DIM), i, row_vec(norm_ffn[i]), row_vec(norm_ple[i]),
                     row_vec(norm_final), mlp_w1[i].astype(BF16), mlp_w2[i].astype(BF16),
                     ple_gate_w[i].astype(BF16), ple_w[i].astype(BF16),
                     final_norm=(i == depth - 1))
    return x.reshape(bsz, seqlen, D_MODEL)
```

```python
import functools
import math

import jax
import jax.numpy as jnp
from jax import lax
from jax.experimental import pallas as pl
from jax.experimental.pallas import tpu as pltpu

D_MODEL = 1024
D_FF = 4 * D_MODEL
PLE_DIM = 256
NORM_EPS = 1e-6
N_MIXERS = 3
SSM_GROUP = 16
SSM_GROUPS = D_MODEL // SSM_GROUP
SSM_STATE = 64
CONV_WIDTH = 3
FOX_HEAD_DIM = 64
FOX_HEADS = D_MODEL // FOX_HEAD_DIM
FOX_PAIRS = FOX_HEADS // 2
LOG2_E = math.log2(math.e)
FOX_SCORE_SCALE = FOX_HEAD_DIM ** -0.5 * LOG2_E
FOX_BIAS_LANES = 6
LANES = 128
SUBLANES = 8
VMEM_LIMIT = 56 * 1024 * 1024

ROW_BLOCK = 512
FFN_ROW_BLOCK = 1024
FF_CHUNK = 1024
ATTN_Q_BLOCK = 256
ATTN_LOOKAHEAD = 2
ATTN_SLOTS = ATTN_LOOKAHEAD + 2

SSM_STEP = 8
SSM_FOLD = SSM_STEP * LANES
SLABS = D_MODEL // LANES
SLAB_GROUPS = LANES // SSM_GROUP
SLAB_STATE = SLAB_GROUPS * SSM_STATE
SCAN_TIME_BLOCK = 1024
INTERLEAVE_TIME_BLOCK = 128
GLU_TIME_BLOCK = 128

BF16 = jnp.bfloat16
F32 = jnp.float32


def _rms(x, gain):
    return x * lax.rsqrt(jnp.mean(x * x, axis=-1, keepdims=True) + NORM_EPS) * gain


def _dot(a, b):
    return jnp.dot(a, b, preferred_element_type=F32)


def _params(n_axes):
    return pltpu.CompilerParams(dimension_semantics=("arbitrary",) * n_axes,
                                vmem_limit_bytes=VMEM_LIMIT)


def _resident(shape):
    nd = len(shape)
    return pl.BlockSpec(shape, lambda *_: (0,) * nd, pipeline_mode=pl.Buffered(1))


def _ffn_ple_kernel(*refs, final_norm, pre_proj):
    if pre_proj:
        (x_ref, p_ref, gf_ref, gp_ref, gl_ref, w1_ref, w2_ref, wg_ref, wp_ref, mix_ref, wo_ref,
         o_ref, h_ref) = refs
        mix = jnp.concatenate([mix_ref[j] for j in range(mix_ref.shape[0])], axis=1)
        o_ref[...] = x_ref[...] + _dot(mix, wo_ref[...])
        x_in = o_ref
    else:
        (x_ref, p_ref, gf_ref, gp_ref, gl_ref, w1_ref, w2_ref, wg_ref, wp_ref,
         o_ref, h_ref) = refs
        x_in = x_ref
    h_ref[...] = _rms(x_in[...], gf_ref[...]).astype(BF16)

    def mlp_chunk(col):
        a = jnp.maximum(_dot(h_ref[...], w1_ref[:, pl.ds(col, FF_CHUNK)]), 0.0)
        return _dot((a * a).astype(BF16), w2_ref[pl.ds(col, FF_CHUNK), :])

    o_ref[...] = x_in[...] + mlp_chunk(0)

    def chunk(c, carry):
        o_ref[...] += mlp_chunk(pl.multiple_of(c * FF_CHUNK, FF_CHUNK))
        return carry

    lax.fori_loop(1, D_FF // FF_CHUNK, chunk, 0)
    x = o_ref[...]
    gate = jax.nn.sigmoid(_dot(_rms(x, gp_ref[...]).astype(BF16), wg_ref[...]))
    x = x + _dot(p_ref[...].astype(BF16), wp_ref[...]) * gate
    if final_norm:
        x = _rms(x, gl_ref[...])
    o_ref[...] = x


def _ffn_ple(x, p, layer, g_ffn, g_ple, g_last, w1, w2, wg, wp, final_norm, mix=None, wo=None):
    n = x.shape[0]
    row = lambda w: pl.BlockSpec((FFN_ROW_BLOCK, w), lambda i: (i, 0))
    in_specs = [row(D_MODEL),
                pl.BlockSpec((None, FFN_ROW_BLOCK, PLE_DIM), lambda i: (layer, i, 0)),
                _resident((1, D_MODEL)), _resident((1, D_MODEL)), _resident((1, D_MODEL)),
                _resident((D_MODEL, D_FF)), _resident((D_FF, D_MODEL)),
                _resident((D_MODEL, D_MODEL)), _resident((PLE_DIM, D_MODEL))]
    args = [x, p, g_ffn, g_ple, g_last, w1, w2, wg, wp]
    if mix is not None:
        _, slabs, seqlen, _ = mix.shape
        tblocks = seqlen // FFN_ROW_BLOCK
        in_specs += [pl.BlockSpec((None, slabs, FFN_ROW_BLOCK, LANES),
                                  lambda i: (i // tblocks, 0, i % tblocks, 0)),
                     _resident((D_MODEL, D_MODEL))]
        args += [mix, wo]
    return pl.pallas_call(
        functools.partial(_ffn_ple_kernel, final_norm=final_norm, pre_proj=mix is not None),
        grid=(n // FFN_ROW_BLOCK,),
        in_specs=in_specs,
        out_specs=row(D_MODEL),
        out_shape=jax.ShapeDtypeStruct((n, D_MODEL), F32),
        scratch_shapes=[pltpu.VMEM((FFN_ROW_BLOCK, D_MODEL), BF16)],
        compiler_params=_params(1),
        name="ffn_ple",
    )(*args)


def _s5_interleave_kernel(x_ref, g_ref, o_ref):
    for b in range(SUBLANES):
        o_ref[:, b, :] = _rms(x_ref[b], g_ref[...])


def _s5_interleave(x4, gain):
    groups, _, seqlen, _ = x4.shape
    tb = INTERLEAVE_TIME_BLOCK
    return pl.pallas_call(
        _s5_interleave_kernel, grid=(groups, seqlen // tb),
        in_specs=[pl.BlockSpec((None, SUBLANES, tb, D_MODEL), lambda g, t: (g, 0, t, 0)),
                  _resident((1, D_MODEL))],
        out_specs=pl.BlockSpec((None, tb, SUBLANES, D_MODEL), lambda g, t: (g, t, 0, 0)),
        out_shape=jax.ShapeDtypeStruct((groups, seqlen, SUBLANES, D_MODEL), F32),
        compiler_params=_params(2), name="s5_interleave",
    )(x4, gain)


def _s5_scan_kernel(u_ref, bs_ref, cs_ref, ds_ref, a_ref, y_ref, bu_ref, xs_ref, st_ref):
    steps = u_ref.shape[0]
    rows = steps * SUBLANES

    @pl.when(pl.program_id(2) == 0)
    def _():
        st_ref[...] = jnp.zeros_like(st_ref)

    u = jnp.concatenate([u_ref[:, i].reshape(rows, LANES).astype(BF16)
                         for i in range(SSM_STEP)], axis=1)
    bu_ref[...] = _dot(u, bs_ref[...])
    a_re = jnp.broadcast_to(a_ref[0:1, :], (SUBLANES, SLAB_STATE))
    a_im = jnp.broadcast_to(a_ref[1:2, :], (SUBLANES, SLAB_STATE))

    def step(k, carry):
        s_re, s_im = carry
        r0 = pl.multiple_of(k * SUBLANES, SUBLANES)
        xs_ref[pl.ds(r0, SUBLANES), 0:SLAB_STATE] = s_re
        xs_ref[pl.ds(r0, SUBLANES), SLAB_STATE:] = s_im
        b_re = bu_ref[pl.ds(r0, SUBLANES), 0:SLAB_STATE]
        b_im = bu_ref[pl.ds(r0, SUBLANES), SLAB_STATE:]
        return (a_re * s_re - a_im * s_im + b_re, a_re * s_im + a_im * s_re + b_im)

    s_re, s_im = lax.fori_loop(0, steps, step, (st_ref[0], st_ref[1]))
    st_ref[0] = s_re
    st_ref[1] = s_im
    y = _dot(xs_ref[...].astype(BF16), cs_ref[...]) + _dot(u, ds_ref[...])
    for j in range(SSM_STEP):
        y_ref[:, j] = y[:, j * LANES:(j + 1) * LANES].reshape(steps, SUBLANES, LANES)


def _s5_scan(u5, bs, cs, ds, a_pow):
    groups, total_steps = u5.shape[0], u5.shape[1]
    steps = min(SCAN_TIME_BLOCK // SSM_STEP, total_steps)
    rows = steps * SUBLANES
    act = pl.BlockSpec((None, steps, SSM_STEP, SUBLANES, LANES), lambda s, g, t: (g, t, 0, 0, s))
    per_slab = lambda *shape: pl.BlockSpec((None,) + shape, lambda s, g, t: (s,) + (0,) * len(shape))
    return pl.pallas_call(
        _s5_scan_kernel, grid=(SLABS, groups, total_steps // steps),
        in_specs=[act, per_slab(SSM_FOLD, 2 * SLAB_STATE), per_slab(2 * SLAB_STATE, SSM_FOLD),
                  per_slab(SSM_FOLD, SSM_FOLD), per_slab(2, SLAB_STATE)],
        out_specs=act, out_shape=jax.ShapeDtypeStruct(u5.shape, F32),
        scratch_shapes=[pltpu.VMEM((rows, 2 * SLAB_STATE), F32),
                        pltpu.VMEM((rows, 2 * SLAB_STATE), F32),
                        pltpu.VMEM((2, SUBLANES, SLAB_STATE), F32)],
        compiler_params=_params(3), name="s5_scan",
    )(u5, bs, cs, ds, a_pow)


def _s5_glu_kernel(x_ref, y_ref, g_ref, d_ref, wa_ref, wg_ref, o_ref, ybuf_ref):
    tb = x_ref.shape[1]
    for b in range(SUBLANES):
        ybuf_ref[b * tb:(b + 1) * tb, :] = y_ref[:, b, :]
    x = x_ref[...].reshape(SUBLANES * tb, D_MODEL)
    u = _rms(x, g_ref[...])
    z = jax.nn.gelu(ybuf_ref[...] + d_ref[...] * u).astype(BF16)
    out = x + _dot(z, wa_ref[...]) * jax.nn.sigmoid(_dot(z, wg_ref[...]))
    o_ref[...] = out.reshape(SUBLANES, tb, D_MODEL)


def _s5_glu(x4, y4, gain, d_skip, wa, wg):
    groups, _, seqlen, _ = x4.shape
    tb = GLU_TIME_BLOCK
    nat = pl.BlockSpec((None, SUBLANES, tb, D_MODEL), lambda g, t: (g, 0, t, 0))
    return pl.pallas_call(
        _s5_glu_kernel, grid=(groups, seqlen // tb),
        in_specs=[nat, pl.BlockSpec((None, tb, SUBLANES, D_MODEL), lambda g, t: (g, t, 0, 0)),
                  _resident((1, D_MODEL)), _resident((1, D_MODEL)),
                  _resident((D_MODEL, D_MODEL)), _resident((D_MODEL, D_MODEL))],
        out_specs=nat, out_shape=jax.ShapeDtypeStruct(x4.shape, F32),
        scratch_shapes=[pltpu.VMEM((SUBLANES * tb, D_MODEL), F32)],
        compiler_params=_params(2), name="s5_glu",
    )(x4, y4, gain, d_skip, wa, wg)


def _s5_weights(lam_re, lam_im, log_dt, b_re, b_im, c_re, c_im):
    hp = lax.Precision.HIGHEST
    S, G, H, P = SSM_STEP, SSM_GROUPS, SSM_GROUP, SSM_STATE
    dt = jnp.exp(log_dt)[:, None]

    def powers(n):
        n = n.astype(F32)[:, None, None]
        mag = jnp.exp(n * (lam_re * dt))
        return mag * jnp.cos(n * (lam_im * dt)), mag * jnp.sin(n * (lam_im * dt))

    pw_re, pw_im = powers(jnp.arange(S + 1))
    rev_re, rev_im = powers(S - 1 - jnp.arange(S))
    nr, ni = pw_re[1] - 1.0, pw_im[1]
    den = lam_re * lam_re + lam_im * lam_im
    coef_re = (nr * lam_re + ni * lam_im) / den
    coef_im = (ni * lam_re - nr * lam_im) / den
    bb_re = coef_re[..., None] * b_re - coef_im[..., None] * b_im
    bb_im = coef_re[..., None] * b_im + coef_im[..., None] * b_re
    ca_re = c_re[None] * pw_re[:, :, None, :] - c_im[None] * pw_im[:, :, None, :]
    ca_im = c_re[None] * pw_im[:, :, None, :] + c_im[None] * pw_re[:, :, None, :]
    bs_re = rev_re[:, :, :, None] * bb_re[None] - rev_im[:, :, :, None] * bb_im[None]
    bs_im = rev_re[:, :, :, None] * bb_im[None] + rev_im[:, :, :, None] * bb_re[None]
    kern = (jnp.einsum('tghp,gpk->tghk', ca_re[:S], bb_re, precision=hp)
            - jnp.einsum('tghp,gpk->tghk', ca_im[:S], bb_im, precision=hp))
    lag = jnp.arange(S)[None, :] - jnp.arange(S)[:, None]
    toep = jnp.where((lag >= 0)[:, :, None, None, None], kern[jnp.maximum(lag, 0)], 0.0)

    def block_diag(t):
        na, nb, _, nx, ny = t.shape
        t = t.reshape(na, nb, SLABS, SLAB_GROUPS, nx, ny).transpose(2, 1, 3, 5, 0, 4)
        compact = t.reshape(SLABS, nb * SLAB_GROUPS * ny, na * nx)
        col = jnp.arange(na * SLAB_GROUPS * nx)
        src = (col // (SLAB_GROUPS * nx)) * nx + col % nx
        widen = (jnp.arange(na * nx)[:, None] == src[None, :]).astype(F32)
        row_group = (jnp.arange(nb * SLAB_GROUPS * ny) // ny) % SLAB_GROUPS
        col_group = (col // nx) % SLAB_GROUPS
        wide = jnp.einsum('srk,kc->src', compact, widen, precision=hp)
        return jnp.where(row_group[:, None] == col_group[None, :], wide, 0.0).astype(BF16)

    bs = block_diag(jnp.stack([bs_re, bs_im], axis=0))
    cs = block_diag(jnp.stack([ca_re[1:], -ca_im[1:]], axis=1))
    ds = block_diag(toep.transpose(1, 0, 2, 3, 4))
    a_pow = jnp.stack([pw_re[S].reshape(SLABS, SLAB_STATE),
                       pw_im[S].reshape(SLABS, SLAB_STATE)], axis=1)
    return bs, cs, ds, a_pow


def _s5_mixer(x, bsz, seqlen, gain, d_skip, weights, wa, wg):
    groups = bsz // SUBLANES
    x4 = x.reshape(groups, SUBLANES, seqlen, D_MODEL)
    h = _s5_interleave(x4, gain)
    folded = (groups, seqlen // SSM_STEP, SSM_STEP, SUBLANES, D_MODEL)
    y = _s5_scan(h.reshape(folded), *weights)
    out = _s5_glu(x4, y.reshape(h.shape), gain, d_skip, wa, wg)
    return out.reshape(bsz * seqlen, D_MODEL)


def _conv_kernel(x_ref, g_ref, wb_ref, wc_ref, wv_ref, cw_ref, wo_ref, o_ref, z_ref):
    t = pl.program_id(1)
    rows = x_ref.shape[0]

    @pl.when(t == 0)
    def _():
        z_ref[0:SUBLANES, :] = jnp.zeros((SUBLANES, D_MODEL), F32)

    x = x_ref[...]
    h = _rms(x, g_ref[...]).astype(BF16)
    z_ref[SUBLANES:, :] = _dot(h, wc_ref[...]) * _dot(h, wv_ref[...])
    conv = cw_ref[CONV_WIDTH - 1:CONV_WIDTH, :] * z_ref[SUBLANES:, :]
    for tap in range(CONV_WIDTH - 1):
        back = CONV_WIDTH - 1 - tap
        conv = conv + cw_ref[tap:tap + 1, :] * z_ref[SUBLANES - back:SUBLANES - back + rows, :]
    z_ref[0:SUBLANES, :] = z_ref[rows:rows + SUBLANES, :]
    gated = (_dot(h, wb_ref[...]) * conv).astype(BF16)
    o_ref[...] = x + _dot(gated, wo_ref[...])


def _conv_mixer(x, bsz, seqlen, gain, wb, wc, wv, conv_w, wo):
    tblocks = seqlen // ROW_BLOCK
    row = pl.BlockSpec((ROW_BLOCK, D_MODEL), lambda b, t: (b * tblocks + t, 0))
    sq = _resident((D_MODEL, D_MODEL))
    return pl.pallas_call(
        _conv_kernel, grid=(bsz, tblocks),
        in_specs=[row, _resident((1, D_MODEL)), sq, sq, sq,
                  _resident((CONV_WIDTH, D_MODEL)), sq],
        out_specs=row, out_shape=jax.ShapeDtypeStruct(x.shape, F32),
        scratch_shapes=[pltpu.VMEM((ROW_BLOCK + SUBLANES, D_MODEL), F32)],
        compiler_params=_params(2), name="conv_mixer",
    )(x, gain, wb, wc, wv, conv_w, wo)


def _split3(x):
    hi = x.astype(BF16)
    rest = x - hi.astype(F32)
    mid = rest.astype(BF16)
    return hi, mid, (rest - mid.astype(F32)).astype(BF16)


def _fox_proj_kernel(x_ref, g_ref, wq_ref, wk_ref, wv_ref, wf_ref, bf_ref,
                     q_ref, k_ref, v_ref, qb_ref, kb_ref, carry_ref):
    t = pl.program_id(1)
    rows = x_ref.shape[0]

    @pl.when(t == 0)
    def _():
        carry_ref[...] = jnp.zeros_like(carry_ref)

    h = _rms(x_ref[...], g_ref[...]).astype(BF16)
    for w_ref, out_ref, mult in ((wq_ref, q_ref, FOX_SCORE_SCALE), (wk_ref, k_ref, None),
                                 (wv_ref, v_ref, None)):
        proj = _dot(h, w_ref[...])
        proj = (proj if mult is None else proj * mult).astype(BF16)
        for j in range(FOX_PAIRS):
            out_ref[j] = proj[:, j * LANES:(j + 1) * LANES]
    logit = _dot(h, wf_ref[...]) + bf_ref[...]
    log_f = jnp.minimum(logit, 0.0) - jnp.log1p(jnp.exp(-jnp.abs(logit)))
    tri = (lax.broadcasted_iota(jnp.int32, (rows, rows), 0)
           >= lax.broadcasted_iota(jnp.int32, (rows, rows), 1)).astype(BF16)
    cum = sum(_dot(tri, piece) for piece in _split3(log_f)) + carry_ref[0:1, :]
    carry_ref[0:1, :] = cum[rows - 1:rows, :]
    src = lax.broadcasted_iota(jnp.int32, (LANES, LANES), 0)
    dst = lax.broadcasted_iota(jnp.int32, (LANES, LANES), 1)
    place = lambda j: ((dst == FOX_BIAS_LANES * src + j) & (src < FOX_HEADS)).astype(BF16)
    pieces = _split3(cum * LOG2_E)
    lane = lax.broadcasted_iota(jnp.int32, (1, LANES), 1)
    slot = lane % FOX_BIAS_LANES
    used = lane < FOX_BIAS_LANES * FOX_HEADS
    ones_q = (used & (slot >= 3)).astype(F32)
    ones_k = (used & (slot < 3)).astype(F32)
    qb_ref[...] = (sum(_dot(pieces[j], place(j)) for j in range(3)) + ones_q).astype(BF16)
    kb_ref[...] = (ones_k - sum(_dot(pieces[j], place(3 + j)) for j in range(3))).astype(BF16)


def _fox_proj(x, bsz, seqlen, gain, wq, wk, wv, wf, bf):
    tblocks = seqlen // ROW_BLOCK
    idx = lambda b, t: (b * tblocks + t, 0)
    row = pl.BlockSpec((ROW_BLOCK, D_MODEL), idx)
    bias = pl.BlockSpec((ROW_BLOCK, LANES), idx)
    pairs = pl.BlockSpec((None, FOX_PAIRS, ROW_BLOCK, LANES), lambda b, t: (b, 0, t, 0))
    sq = _resident((D_MODEL, D_MODEL))
    qkv = jax.ShapeDtypeStruct((bsz, FOX_PAIRS, seqlen, LANES), BF16)
    bias_shape = jax.ShapeDtypeStruct((x.shape[0], LANES), BF16)
    return pl.pallas_call(
        _fox_proj_kernel, grid=(bsz, tblocks),
        in_specs=[row, _resident((1, D_MODEL)), sq, sq, sq,
                  _resident((D_MODEL, LANES)), _resident((1, LANES))],
        out_specs=[pairs, pairs, pairs, bias, bias],
        out_shape=[qkv, qkv, qkv, bias_shape, bias_shape],
        scratch_shapes=[pltpu.VMEM((SUBLANES, LANES), F32)],
        compiler_params=_params(2), name="fox_proj",
    )(x, gain, wq, wk, wv, wf, bf)


def _fox_attn_kernel(q_ref, k_ref, v_ref, qb_ref, kb_ref, o_ref, kk_ref, vt_ref, s_ref, p_ref):
    pair = pl.program_id(1)
    seqlen = q_ref.shape[1]
    tq = tk = ATTN_Q_BLOCK
    lane = lax.broadcasted_iota(jnp.int32, (1, LANES), 1)
    low = lane < FOX_HEAD_DIM
    keep = (lax.broadcasted_iota(jnp.int32, (tk, tq), 1)
            >= lax.broadcasted_iota(jnp.int32, (tk, tq), 0))
    contract_last = (((1,), (1,)), ((), ()))
    kk_ref[:, 0:LANES] = k_ref[0]
    kk_ref[:, LANES:] = kb_ref[0]
    vt_ref[...] = v_ref[0].astype(F32).T.astype(BF16)
    units = [(qb, e) for qb in range(seqlen // tq) for e in range(2)]

    def score_unit(u):
        qb, e = units[u]
        q0 = qb * tq
        q = q_ref[0, q0:q0 + tq, :]
        q_bias = qb_ref[0, q0:q0 + tq, :]
        first = FOX_BIAS_LANES * (2 * pair + e)
        mine = low if e == 0 else jnp.logical_not(low)
        own_bias = (lane >= first) & (lane < first + FOX_BIAS_LANES)
        qq = jnp.concatenate([jnp.where(mine, q, jnp.zeros_like(q)),
                              jnp.where(own_bias, q_bias, jnp.zeros_like(q_bias))], axis=1)
        scores = lambda r0, rows: lax.dot_general(
            kk_ref[r0:r0 + rows, :], qq, contract_last, preferred_element_type=F32)
        sc = s_ref.at[u % ATTN_SLOTS]
        if qb > 0:
            sc[0:q0, :] = scores(0, q0)
        sc[q0:q0 + tk, :] = jnp.where(keep, scores(q0, tk), -jnp.inf)

    def softmax_unit(u):
        qb, e = units[u]
        kv = qb * tq + tk
        sc, pr = s_ref.at[u % ATTN_SLOTS], p_ref.at[u % ATTN_SLOTS]
        m = sc[0:SUBLANES, :]
        for r in range(SUBLANES, kv, SUBLANES):
            m = jnp.maximum(m, sc[r:r + SUBLANES, :])
        m = jnp.max(m, axis=0, keepdims=True)
        l = jnp.zeros((SUBLANES, tq), F32)
        for r in range(0, kv, 2 * SUBLANES):
            p = jnp.exp2(sc[r:r + 2 * SUBLANES, :] - m)
            pr[r:r + 2 * SUBLANES, :] = p.astype(BF16)
            l = l + p[0:SUBLANES] + p[SUBLANES:]
        l = jnp.sum(l, axis=0, keepdims=True)
        out_t = _dot(vt_ref[e * FOX_HEAD_DIM:(e + 1) * FOX_HEAD_DIM, 0:kv], pr[0:kv, :])
        return out_t, l

    for u in range(ATTN_LOOKAHEAD):
        score_unit(u)
    pending = []
    for u, (qb, e) in enumerate(units):
        if u + ATTN_LOOKAHEAD < len(units):
            score_unit(u + ATTN_LOOKAHEAD)
        if e == 0 and pending:
            done_qb, parts = pending.pop()
            o_ref[0, done_qb * tq:(done_qb + 1) * tq, :] = jnp.concatenate(
                [out_t / l for out_t, l in parts], axis=0).T.astype(BF16)
        if e == 0:
            pending.append((qb, []))
        pending[-1][1].append(softmax_unit(u))
    done_qb, parts = pending.pop()
    o_ref[0, done_qb * tq:(done_qb + 1) * tq, :] = jnp.concatenate(
        [out_t / l for out_t, l in parts], axis=0).T.astype(BF16)


def _fox_attn(q, k, v, q_bias, k_bias):
    bsz, _, seqlen, _ = q.shape
    seq = pl.BlockSpec((None, 1, seqlen, LANES), lambda b, j: (b, j, 0, 0))
    shared = pl.BlockSpec((1, seqlen, LANES), lambda b, j: (b, 0, 0))
    return pl.pallas_call(
        _fox_attn_kernel, grid=(bsz, FOX_PAIRS),
        in_specs=[seq, seq, seq, shared, shared],
        out_specs=seq, out_shape=jax.ShapeDtypeStruct(q.shape, BF16),
        scratch_shapes=[pltpu.VMEM((seqlen, 2 * LANES), BF16),
                        pltpu.VMEM((LANES, seqlen), BF16),
                        pltpu.VMEM((ATTN_SLOTS, seqlen, ATTN_Q_BLOCK), F32),
                        pltpu.VMEM((ATTN_SLOTS, seqlen, ATTN_Q_BLOCK), BF16)],
        compiler_params=_params(2), name="fox_attn",
    )(q, k, v, q_bias, k_bias)


def _fox_mixer(x, bsz, seqlen, gain, w_in, b_f):
    wq, wk, wv = (w_in[:, i * D_MODEL:(i + 1) * D_MODEL].astype(BF16) for i in range(3))
    wf = jnp.pad(w_in[:, 3 * D_MODEL:], ((0, 0), (0, LANES - FOX_HEADS))).astype(BF16)
    bf = jnp.pad(b_f, (0, LANES - FOX_HEADS)).reshape(1, LANES)
    q, k, v, q_bias, k_bias = _fox_proj(x, bsz, seqlen, gain, wq, wk, wv, wf, bf)
    bias_shape = (bsz, seqlen, LANES)
    return _fox_attn(q, k, v, q_bias.reshape(bias_shape), k_bias.reshape(bias_shape))


def kernel(x, p, norm_mix, norm_ffn, norm_ple, norm_final, ssm_lam_re, ssm_lam_im, ssm_log_dt,
           ssm_b_re, ssm_b_im, ssm_c_re, ssm_c_im, ssm_d, ssm_w_glu, conv_w_in, conv_w,
           conv_w_out, fox_w_in, fox_b_f, fox_w_out, mlp_w1, mlp_w2, ple_w, ple_gate_w):
    bsz, seqlen, _ = x.shape
    depth = p.shape[0]
    assert seqlen % FFN_ROW_BLOCK == 0 and seqlen % ATTN_Q_BLOCK == 0 and bsz % SUBLANES == 0
    n = bsz * seqlen
    row_vec = lambda v: v.reshape(1, D_MODEL)
    x = x.reshape(n, D_MODEL)
    for i in range(depth):
        kind, slot = i % N_MIXERS, i // N_MIXERS
        gain = row_vec(norm_mix[i])
        pending = {}
        if kind == 0:
            weights = _s5_weights(ssm_lam_re[slot], ssm_lam_im[slot], ssm_log_dt[slot],
                                  ssm_b_re[slot], ssm_b_im[slot], ssm_c_re[slot], ssm_c_im[slot])
            w_glu = ssm_w_glu[slot].astype(BF16)
            x = _s5_mixer(x, bsz, seqlen, gain, row_vec(ssm_d[slot]), weights,
                          w_glu[:, :D_MODEL], w_glu[:, D_MODEL:])
        elif kind == 1:
            w_in = conv_w_in[slot].astype(BF16)
            x = _conv_mixer(x, bsz, seqlen, gain, w_in[:, :D_MODEL],
                            w_in[:, D_MODEL:2 * D_MODEL], w_in[:, 2 * D_MODEL:],
                            conv_w[slot], conv_w_out[slot].astype(BF16))
        else:
            pending = dict(mix=_fox_mixer(x, bsz, seqlen, gain, fox_w_in[slot], fox_b_f[slot]),
                           wo=fox_w_out[slot].astype(BF16))
        x = _ffn_ple(x, p.reshape(depth, n, PLE_DIM), i, row_vec(norm_ffn[i]), row_vec(norm_ple[i]),
                     row_vec(norm_final), mlp_w1[i].astype(BF16), mlp_w2[i].astype(BF16),
                     ple_gate_w[i].astype(BF16), ple_w[i].astype(BF16),
                     final_norm=(i == depth - 1), **pending)
    return x.reshape(bsz, seqlen, D_MODEL)
```

```python
import functools
import math

import jax
import jax.numpy as jnp
from jax import lax
from jax.experimental import pallas as pl
from jax.experimental.pallas import tpu as pltpu

D_MODEL = 1024
D_FF = 4 * D_MODEL
PLE_DIM = 256
NORM_EPS = 1e-6
N_MIXERS = 3
SSM_GROUP = 16
SSM_GROUPS = D_MODEL // SSM_GROUP
SSM_STATE = 64
CONV_WIDTH = 3
FOX_HEAD_DIM = 64
FOX_HEADS = D_MODEL // FOX_HEAD_DIM
FOX_PAIRS = FOX_HEADS // 2
LOG2_E = math.log2(math.e)
FOX_SCORE_SCALE = FOX_HEAD_DIM ** -0.5 * LOG2_E
FOX_BIAS_LANES = 6
LANES = 128
SUBLANES = 8
VMEM_LIMIT = 56 * 1024 * 1024

ROW_BLOCK = 512
FFN_ROW_BLOCK = 1024
FF_CHUNK = 1024
ATTN_Q_BLOCK = 256
ATTN_LOOKAHEAD = 2
ATTN_SLOTS = ATTN_LOOKAHEAD + 2

SSM_STEP = 8
SSM_FOLD = SSM_STEP * LANES
SLABS = D_MODEL // LANES
SLAB_GROUPS = LANES // SSM_GROUP
SLAB_STATE = SLAB_GROUPS * SSM_STATE
SCAN_TIME_BLOCK = 1024
INTERLEAVE_TIME_BLOCK = 128
GLU_TIME_BLOCK = 128

BF16 = jnp.bfloat16
F32 = jnp.float32


def _rms(x, gain):
    return x * lax.rsqrt(jnp.mean(x * x, axis=-1, keepdims=True) + NORM_EPS) * gain


def _dot(a, b):
    return jnp.dot(a, b, preferred_element_type=F32)


def _params(n_axes):
    return pltpu.CompilerParams(dimension_semantics=("arbitrary",) * n_axes,
                                vmem_limit_bytes=VMEM_LIMIT)


def _resident(shape):
    nd = len(shape)
    return pl.BlockSpec(shape, lambda *_: (0,) * nd, pipeline_mode=pl.Buffered(1))


def _ffn_ple_kernel(*refs, final_norm, pre_proj):
    if pre_proj:
        (x_ref, p_ref, gf_ref, gp_ref, gl_ref, w1_ref, w2_ref, wg_ref, wp_ref, mix_ref, wo_ref,
         o_ref, h_ref) = refs
        mix = jnp.concatenate([mix_ref[j] for j in range(mix_ref.shape[0])], axis=1)
        o_ref[...] = x_ref[...] + _dot(mix, wo_ref[...])
        x_in = o_ref
    else:
        (x_ref, p_ref, gf_ref, gp_ref, gl_ref, w1_ref, w2_ref, wg_ref, wp_ref,
         o_ref, h_ref) = refs
        x_in = x_ref
    h_ref[...] = _rms(x_in[...], gf_ref[...]).astype(BF16)

    def mlp_chunk(col):
        a = jnp.maximum(_dot(h_ref[...], w1_ref[:, pl.ds(col, FF_CHUNK)]), 0.0)
        return _dot((a * a).astype(BF16), w2_ref[pl.ds(col, FF_CHUNK), :])

    o_ref[...] = x_in[...] + mlp_chunk(0)

    def chunk(c, carry):
        o_ref[...] += mlp_chunk(pl.multiple_of(c * FF_CHUNK, FF_CHUNK))
        return carry

    lax.fori_loop(1, D_FF // FF_CHUNK, chunk, 0)
    x = o_ref[...]
    gate = jax.nn.sigmoid(_dot(_rms(x, gp_ref[...]).astype(BF16), wg_ref[...]))
    x = x + _dot(p_ref[...].astype(BF16), wp_ref[...]) * gate
    if final_norm:
        x = _rms(x, gl_ref[...])
    o_ref[...] = x


def _ffn_ple(x, p, layer, g_ffn, g_ple, g_last, w1, w2, wg, wp, final_norm, mix=None, wo=None):
    n = x.shape[0]
    row = lambda w: pl.BlockSpec((FFN_ROW_BLOCK, w), lambda i: (i, 0))
    in_specs = [row(D_MODEL),
                pl.BlockSpec((None, FFN_ROW_BLOCK, PLE_DIM), lambda i: (layer, i, 0)),
                _resident((1, D_MODEL)), _resident((1, D_MODEL)), _resident((1, D_MODEL)),
                _resident((D_MODEL, D_FF)), _resident((D_FF, D_MODEL)),
                _resident((D_MODEL, D_MODEL)), _resident((PLE_DIM, D_MODEL))]
    args = [x, p, g_ffn, g_ple, g_last, w1, w2, wg, wp]
    if mix is not None:
        _, slabs, seqlen, _ = mix.shape
        tblocks = seqlen // FFN_ROW_BLOCK
        in_specs += [pl.BlockSpec((None, slabs, FFN_ROW_BLOCK, LANES),
                                  lambda i: (i // tblocks, 0, i % tblocks, 0)),
                     _resident((D_MODEL, D_MODEL))]
        args += [mix, wo]
    return pl.pallas_call(
        functools.partial(_ffn_ple_kernel, final_norm=final_norm, pre_proj=mix is not None),
        grid=(n // FFN_ROW_BLOCK,),
        in_specs=in_specs,
        out_specs=row(D_MODEL),
        out_shape=jax.ShapeDtypeStruct((n, D_MODEL), F32),
        scratch_shapes=[pltpu.VMEM((FFN_ROW_BLOCK, D_MODEL), BF16)],
        compiler_params=_params(1),
        name="ffn_ple",
    )(*args)


def _s5_interleave_kernel(x_ref, g_ref, o_ref):
    for b in range(SUBLANES):
        o_ref[:, b, :] = _rms(x_ref[b], g_ref[...])


def _s5_interleave(x4, gain):
    groups, _, seqlen, _ = x4.shape
    tb = INTERLEAVE_TIME_BLOCK
    return pl.pallas_call(
        _s5_interleave_kernel, grid=(groups, seqlen // tb),
        in_specs=[pl.BlockSpec((None, SUBLANES, tb, D_MODEL), lambda g, t: (g, 0, t, 0)),
                  _resident((1, D_MODEL))],
        out_specs=pl.BlockSpec((None, tb, SUBLANES, D_MODEL), lambda g, t: (g, t, 0, 0)),
        out_shape=jax.ShapeDtypeStruct((groups, seqlen, SUBLANES, D_MODEL), F32),
        compiler_params=_params(2), name="s5_interleave",
    )(x4, gain)


def _s5_scan_kernel(u_ref, bs_ref, cs_ref, ds_ref, a_ref, y_ref, bu_ref, xs_ref, st_ref):
    steps = u_ref.shape[0]
    rows = steps * SUBLANES

    @pl.when(pl.program_id(2) == 0)
    def _():
        st_ref[...] = jnp.zeros_like(st_ref)

    u = jnp.concatenate([u_ref[:, i].reshape(rows, LANES).astype(BF16)
                         for i in range(SSM_STEP)], axis=1)
    bu_ref[...] = _dot(u, bs_ref[...])
    a_re = jnp.broadcast_to(a_ref[0:1, :], (SUBLANES, SLAB_STATE))
    a_im = jnp.broadcast_to(a_ref[1:2, :], (SUBLANES, SLAB_STATE))

    def step(k, carry):
        s_re, s_im = carry
        r0 = pl.multiple_of(k * SUBLANES, SUBLANES)
        xs_ref[pl.ds(r0, SUBLANES), 0:SLAB_STATE] = s_re
        xs_ref[pl.ds(r0, SUBLANES), SLAB_STATE:] = s_im
        b_re = bu_ref[pl.ds(r0, SUBLANES), 0:SLAB_STATE]
        b_im = bu_ref[pl.ds(r0, SUBLANES), SLAB_STATE:]
        return (a_re * s_re - a_im * s_im + b_re, a_re * s_im + a_im * s_re + b_im)

    s_re, s_im = lax.fori_loop(0, steps, step, (st_ref[0], st_ref[1]))
    st_ref[0] = s_re
    st_ref[1] = s_im
    y = _dot(xs_ref[...].astype(BF16), cs_ref[...]) + _dot(u, ds_ref[...])
    for j in range(SSM_STEP):
        y_ref[:, j] = y[:, j * LANES:(j + 1) * LANES].reshape(steps, SUBLANES, LANES)


def _s5_scan(u5, bs, cs, ds, a_pow, slot):
    groups, total_steps = u5.shape[0], u5.shape[1]
    steps = min(SCAN_TIME_BLOCK // SSM_STEP, total_steps)
    rows = steps * SUBLANES
    act = pl.BlockSpec((None, steps, SSM_STEP, SUBLANES, LANES), lambda s, g, t: (g, t, 0, 0, s))
    per_slab = lambda *shape: pl.BlockSpec((None, None) + shape,
                                           lambda s, g, t: (slot, s) + (0,) * len(shape))
    return pl.pallas_call(
        _s5_scan_kernel, grid=(SLABS, groups, total_steps // steps),
        in_specs=[act, per_slab(SSM_FOLD, 2 * SLAB_STATE), per_slab(2 * SLAB_STATE, SSM_FOLD),
                  per_slab(SSM_FOLD, SSM_FOLD), per_slab(2, SLAB_STATE)],
        out_specs=act, out_shape=jax.ShapeDtypeStruct(u5.shape, F32),
        scratch_shapes=[pltpu.VMEM((rows, 2 * SLAB_STATE), F32),
                        pltpu.VMEM((rows, 2 * SLAB_STATE), F32),
                        pltpu.VMEM((2, SUBLANES, SLAB_STATE), F32)],
        compiler_params=_params(3), name="s5_scan",
    )(u5, bs, cs, ds, a_pow)


def _s5_glu_kernel(x_ref, y_ref, g_ref, d_ref, wa_ref, wg_ref, o_ref, ybuf_ref):
    tb = x_ref.shape[1]
    for b in range(SUBLANES):
        ybuf_ref[b * tb:(b + 1) * tb, :] = y_ref[:, b, :]
    x = x_ref[...].reshape(SUBLANES * tb, D_MODEL)
    u = _rms(x, g_ref[...])
    z = jax.nn.gelu(ybuf_ref[...] + d_ref[...] * u).astype(BF16)
    out = x + _dot(z, wa_ref[...]) * jax.nn.sigmoid(_dot(z, wg_ref[...]))
    o_ref[...] = out.reshape(SUBLANES, tb, D_MODEL)


def _s5_glu(x4, y4, gain, d_skip, wa, wg):
    groups, _, seqlen, _ = x4.shape
    tb = GLU_TIME_BLOCK
    nat = pl.BlockSpec((None, SUBLANES, tb, D_MODEL), lambda g, t: (g, 0, t, 0))
    return pl.pallas_call(
        _s5_glu_kernel, grid=(groups, seqlen // tb),
        in_specs=[nat, pl.BlockSpec((None, tb, SUBLANES, D_MODEL), lambda g, t: (g, t, 0, 0)),
                  _resident((1, D_MODEL)), _resident((1, D_MODEL)),
                  _resident((D_MODEL, D_MODEL)), _resident((D_MODEL, D_MODEL))],
        out_specs=nat, out_shape=jax.ShapeDtypeStruct(x4.shape, F32),
        scratch_shapes=[pltpu.VMEM((SUBLANES * tb, D_MODEL), F32)],
        compiler_params=_params(2), name="s5_glu",
    )(x4, y4, gain, d_skip, wa, wg)


def _s5_weights(lam_re, lam_im, log_dt, b_re, b_im, c_re, c_im):
    hp = lax.Precision.HIGHEST
    S, G, H, P = SSM_STEP, SSM_GROUPS, SSM_GROUP, SSM_STATE
    dt = jnp.exp(log_dt)[:, None]

    def powers(n):
        n = n.astype(F32)[:, None, None]
        mag = jnp.exp(n * (lam_re * dt))
        return mag * jnp.cos(n * (lam_im * dt)), mag * jnp.sin(n * (lam_im * dt))

    pw_re, pw_im = powers(jnp.arange(S + 1))
    rev_re, rev_im = powers(S - 1 - jnp.arange(S))
    nr, ni = pw_re[1] - 1.0, pw_im[1]
    den = lam_re * lam_re + lam_im * lam_im
    coef_re = (nr * lam_re + ni * lam_im) / den
    coef_im = (ni * lam_re - nr * lam_im) / den
    bb_re = coef_re[..., None] * b_re - coef_im[..., None] * b_im
    bb_im = coef_re[..., None] * b_im + coef_im[..., None] * b_re
    ca_re = c_re[None] * pw_re[:, :, None, :] - c_im[None] * pw_im[:, :, None, :]
    ca_im = c_re[None] * pw_im[:, :, None, :] + c_im[None] * pw_re[:, :, None, :]
    bs_re = rev_re[:, :, :, None] * bb_re[None] - rev_im[:, :, :, None] * bb_im[None]
    bs_im = rev_re[:, :, :, None] * bb_im[None] + rev_im[:, :, :, None] * bb_re[None]
    kern = (jnp.einsum('tghp,gpk->tghk', ca_re[:S], bb_re, precision=hp)
            - jnp.einsum('tghp,gpk->tghk', ca_im[:S], bb_im, precision=hp))
    lag = jnp.arange(S)[None, :] - jnp.arange(S)[:, None]
    toep = jnp.where((lag >= 0)[:, :, None, None, None], kern[jnp.maximum(lag, 0)], 0.0)

    def block_diag(t):
        na, nb, _, nx, ny = t.shape
        t = t.reshape(na, nb, SLABS, SLAB_GROUPS, nx, ny).transpose(2, 1, 3, 5, 0, 4)
        compact = t.reshape(SLABS, nb * SLAB_GROUPS * ny, na * nx)
        col = jnp.arange(na * SLAB_GROUPS * nx)
        src = (col // (SLAB_GROUPS * nx)) * nx + col % nx
        widen = (jnp.arange(na * nx)[:, None] == src[None, :]).astype(F32)
        row_group = (jnp.arange(nb * SLAB_GROUPS * ny) // ny) % SLAB_GROUPS
        col_group = (col // nx) % SLAB_GROUPS
        wide = jnp.einsum('srk,kc->src', compact, widen, precision=hp)
        return jnp.where(row_group[:, None] == col_group[None, :], wide, 0.0).astype(BF16)

    bs = block_diag(jnp.stack([bs_re, bs_im], axis=0))
    cs = block_diag(jnp.stack([ca_re[1:], -ca_im[1:]], axis=1))
    ds = block_diag(toep.transpose(1, 0, 2, 3, 4))
    a_pow = jnp.stack([pw_re[S].reshape(SLABS, SLAB_STATE),
                       pw_im[S].reshape(SLABS, SLAB_STATE)], axis=1)
    return bs, cs, ds, a_pow


def _s5_mixer(x, bsz, seqlen, gain, d_skip, weights, slot, wa, wg):
    groups = bsz // SUBLANES
    x4 = x.reshape(groups, SUBLANES, seqlen, D_MODEL)
    h = _s5_interleave(x4, gain)
    folded = (groups, seqlen // SSM_STEP, SSM_STEP, SUBLANES, D_MODEL)
    y = _s5_scan(h.reshape(folded), *weights, slot)
    out = _s5_glu(x4, y.reshape(h.shape), gain, d_skip, wa, wg)
    return out.reshape(bsz * seqlen, D_MODEL)


def _conv_kernel(x_ref, g_ref, wb_ref, wc_ref, wv_ref, cw_ref, wo_ref, o_ref, z_ref):
    t = pl.program_id(1)
    rows = x_ref.shape[0]

    @pl.when(t == 0)
    def _():
        z_ref[0:SUBLANES, :] = jnp.zeros((SUBLANES, D_MODEL), F32)

    x = x_ref[...]
    h = _rms(x, g_ref[...]).astype(BF16)
    z_ref[SUBLANES:, :] = _dot(h, wc_ref[...]) * _dot(h, wv_ref[...])
    conv = cw_ref[CONV_WIDTH - 1:CONV_WIDTH, :] * z_ref[SUBLANES:, :]
    for tap in range(CONV_WIDTH - 1):
        back = CONV_WIDTH - 1 - tap
        conv = conv + cw_ref[tap:tap + 1, :] * z_ref[SUBLANES - back:SUBLANES - back + rows, :]
    z_ref[0:SUBLANES, :] = z_ref[rows:rows + SUBLANES, :]
    gated = (_dot(h, wb_ref[...]) * conv).astype(BF16)
    o_ref[...] = x + _dot(gated, wo_ref[...])


def _conv_mixer(x, bsz, seqlen, gain, wb, wc, wv, conv_w, wo):
    tblocks = seqlen // ROW_BLOCK
    row = pl.BlockSpec((ROW_BLOCK, D_MODEL), lambda b, t: (b * tblocks + t, 0))
    sq = _resident((D_MODEL, D_MODEL))
    return pl.pallas_call(
        _conv_kernel, grid=(bsz, tblocks),
        in_specs=[row, _resident((1, D_MODEL)), sq, sq, sq,
                  _resident((CONV_WIDTH, D_MODEL)), sq],
        out_specs=row, out_shape=jax.ShapeDtypeStruct(x.shape, F32),
        scratch_shapes=[pltpu.VMEM((ROW_BLOCK + SUBLANES, D_MODEL), F32)],
        compiler_params=_params(2), name="conv_mixer",
    )(x, gain, wb, wc, wv, conv_w, wo)


def _split3(x):
    hi = x.astype(BF16)
    rest = x - hi.astype(F32)
    mid = rest.astype(BF16)
    return hi, mid, (rest - mid.astype(F32)).astype(BF16)


def _fox_proj_kernel(x_ref, g_ref, wq_ref, wk_ref, wv_ref, wf_ref, bf_ref,
                     q_ref, k_ref, v_ref, qb_ref, kb_ref, carry_ref):
    t = pl.program_id(1)
    rows = x_ref.shape[0]

    @pl.when(t == 0)
    def _():
        carry_ref[...] = jnp.zeros_like(carry_ref)

    h = _rms(x_ref[...], g_ref[...]).astype(BF16)
    for w_ref, out_ref, mult in ((wq_ref, q_ref, FOX_SCORE_SCALE), (wk_ref, k_ref, None),
                                 (wv_ref, v_ref, None)):
        proj = _dot(h, w_ref[...])
        proj = (proj if mult is None else proj * mult).astype(BF16)
        for j in range(FOX_PAIRS):
            out_ref[j] = proj[:, j * LANES:(j + 1) * LANES]
    logit = _dot(h, wf_ref[...]) + bf_ref[...]
    log_f = jnp.minimum(logit, 0.0) - jnp.log1p(jnp.exp(-jnp.abs(logit)))
    tri = (lax.broadcasted_iota(jnp.int32, (rows, rows), 0)
           >= lax.broadcasted_iota(jnp.int32, (rows, rows), 1)).astype(BF16)
    cum = sum(_dot(tri, piece) for piece in _split3(log_f)) + carry_ref[0:1, :]
    carry_ref[0:1, :] = cum[rows - 1:rows, :]
    src = lax.broadcasted_iota(jnp.int32, (LANES, LANES), 0)
    dst = lax.broadcasted_iota(jnp.int32, (LANES, LANES), 1)
    place = lambda j: ((dst == FOX_BIAS_LANES * src + j) & (src < FOX_HEADS)).astype(BF16)
    pieces = _split3(cum * LOG2_E)
    lane = lax.broadcasted_iota(jnp.int32, (1, LANES), 1)
    slot = lane % FOX_BIAS_LANES
    used = lane < FOX_BIAS_LANES * FOX_HEADS
    ones_q = (used & (slot >= 3)).astype(F32)
    ones_k = (used & (slot < 3)).astype(F32)
    qb_ref[...] = (sum(_dot(pieces[j], place(j)) for j in range(3)) + ones_q).astype(BF16)
    kb_ref[...] = (ones_k - sum(_dot(pieces[j], place(3 + j)) for j in range(3))).astype(BF16)


def _fox_proj(x, bsz, seqlen, gain, wq, wk, wv, wf, bf):
    tblocks = seqlen // ROW_BLOCK
    idx = lambda b, t: (b * tblocks + t, 0)
    row = pl.BlockSpec((ROW_BLOCK, D_MODEL), idx)
    bias = pl.BlockSpec((ROW_BLOCK, LANES), idx)
    pairs = pl.BlockSpec((None, FOX_PAIRS, ROW_BLOCK, LANES), lambda b, t: (b, 0, t, 0))
    sq = _resident((D_MODEL, D_MODEL))
    qkv = jax.ShapeDtypeStruct((bsz, FOX_PAIRS, seqlen, LANES), BF16)
    bias_shape = jax.ShapeDtypeStruct((x.shape[0], LANES), BF16)
    return pl.pallas_call(
        _fox_proj_kernel, grid=(bsz, tblocks),
        in_specs=[row, _resident((1, D_MODEL)), sq, sq, sq,
                  _resident((D_MODEL, LANES)), _resident((1, LANES))],
        out_specs=[pairs, pairs, pairs, bias, bias],
        out_shape=[qkv, qkv, qkv, bias_shape, bias_shape],
        scratch_shapes=[pltpu.VMEM((SUBLANES, LANES), F32)],
        compiler_params=_params(2), name="fox_proj",
    )(x, gain, wq, wk, wv, wf, bf)


def _fox_attn_kernel(q_ref, k_ref, v_ref, qb_ref, kb_ref, o_ref, kk_ref, vt_ref, s_ref, p_ref):
    pair = pl.program_id(1)
    seqlen = q_ref.shape[1]
    tq = tk = ATTN_Q_BLOCK
    lane = lax.broadcasted_iota(jnp.int32, (1, LANES), 1)
    low = lane < FOX_HEAD_DIM
    keep = (lax.broadcasted_iota(jnp.int32, (tk, tq), 1)
            >= lax.broadcasted_iota(jnp.int32, (tk, tq), 0))
    contract_last = (((1,), (1,)), ((), ()))
    kk_ref[:, 0:LANES] = k_ref[0]
    kk_ref[:, LANES:] = kb_ref[0]
    vt_ref[...] = v_ref[0].astype(F32).T.astype(BF16)
    units = [(qb, e) for qb in range(seqlen // tq) for e in range(2)]

    def score_unit(u):
        qb, e = units[u]
        q0 = qb * tq
        q = q_ref[0, q0:q0 + tq, :]
        q_bias = qb_ref[0, q0:q0 + tq, :]
        first = FOX_BIAS_LANES * (2 * pair + e)
        mine = low if e == 0 else jnp.logical_not(low)
        own_bias = (lane >= first) & (lane < first + FOX_BIAS_LANES)
        qq = jnp.concatenate([jnp.where(mine, q, jnp.zeros_like(q)),
                              jnp.where(own_bias, q_bias, jnp.zeros_like(q_bias))], axis=1)
        scores = lambda r0, rows: lax.dot_general(
            kk_ref[r0:r0 + rows, :], qq, contract_last, preferred_element_type=F32)
        sc = s_ref.at[u % ATTN_SLOTS]
        if qb > 0:
            sc[0:q0, :] = scores(0, q0)
        sc[q0:q0 + tk, :] = jnp.where(keep, scores(q0, tk), -jnp.inf)

    def softmax_unit(u):
        qb, e = units[u]
        kv = qb * tq + tk
        sc, pr = s_ref.at[u % ATTN_SLOTS], p_ref.at[u % ATTN_SLOTS]
        m = sc[0:SUBLANES, :]
        for r in range(SUBLANES, kv, SUBLANES):
            m = jnp.maximum(m, sc[r:r + SUBLANES, :])
        m = jnp.max(m, axis=0, keepdims=True)
        l = jnp.zeros((SUBLANES, tq), F32)
        for r in range(0, kv, 2 * SUBLANES):
            p = jnp.exp2(sc[r:r + 2 * SUBLANES, :] - m)
            pr[r:r + 2 * SUBLANES, :] = p.astype(BF16)
            l = l + p[0:SUBLANES] + p[SUBLANES:]
        l = jnp.sum(l, axis=0, keepdims=True)
        out_t = _dot(vt_ref[e * FOX_HEAD_DIM:(e + 1) * FOX_HEAD_DIM, 0:kv], pr[0:kv, :])
        return out_t, l

    for u in range(ATTN_LOOKAHEAD):
        score_unit(u)
    pending = []
    for u, (qb, e) in enumerate(units):
        if u + ATTN_LOOKAHEAD < len(units):
            score_unit(u + ATTN_LOOKAHEAD)
        if e == 0 and pending:
            done_qb, parts = pending.pop()
            o_ref[0, done_qb * tq:(done_qb + 1) * tq, :] = jnp.concatenate(
                [out_t / l for out_t, l in parts], axis=0).T.astype(BF16)
        if e == 0:
            pending.append((qb, []))
        pending[-1][1].append(softmax_unit(u))
    done_qb, parts = pending.pop()
    o_ref[0, done_qb * tq:(done_qb + 1) * tq, :] = jnp.concatenate(
        [out_t / l for out_t, l in parts], axis=0).T.astype(BF16)


def _fox_attn(q, k, v, q_bias, k_bias):
    bsz, _, seqlen, _ = q.shape
    seq = pl.BlockSpec((None, 1, seqlen, LANES), lambda b, j: (b, j, 0, 0))
    shared = pl.BlockSpec((1, seqlen, LANES), lambda b, j: (b, 0, 0))
    return pl.pallas_call(
        _fox_attn_kernel, grid=(bsz, FOX_PAIRS),
        in_specs=[seq, seq, seq, shared, shared],
        out_specs=seq, out_shape=jax.ShapeDtypeStruct(q.shape, BF16),
        scratch_shapes=[pltpu.VMEM((seqlen, 2 * LANES), BF16),
                        pltpu.VMEM((LANES, seqlen), BF16),
                        pltpu.VMEM((ATTN_SLOTS, seqlen, ATTN_Q_BLOCK), F32),
                        pltpu.VMEM((ATTN_SLOTS, seqlen, ATTN_Q_BLOCK), BF16)],
        compiler_params=_params(2), name="fox_attn",
    )(q, k, v, q_bias, k_bias)


def _fox_mixer(x, bsz, seqlen, gain, w_in, b_f):
    wq, wk, wv = (w_in[:, i * D_MODEL:(i + 1) * D_MODEL].astype(BF16) for i in range(3))
    wf = jnp.pad(w_in[:, 3 * D_MODEL:], ((0, 0), (0, LANES - FOX_HEADS))).astype(BF16)
    bf = jnp.pad(b_f, (0, LANES - FOX_HEADS)).reshape(1, LANES)
    q, k, v, q_bias, k_bias = _fox_proj(x, bsz, seqlen, gain, wq, wk, wv, wf, bf)
    bias_shape = (bsz, seqlen, LANES)
    return _fox_attn(q, k, v, q_bias.reshape(bias_shape), k_bias.reshape(bias_shape))


def kernel(x, p, norm_mix, norm_ffn, norm_ple, norm_final, ssm_lam_re, ssm_lam_im, ssm_log_dt,
           ssm_b_re, ssm_b_im, ssm_c_re, ssm_c_im, ssm_d, ssm_w_glu, conv_w_in, conv_w,
           conv_w_out, fox_w_in, fox_b_f, fox_w_out, mlp_w1, mlp_w2, ple_w, ple_gate_w):
    bsz, seqlen, _ = x.shape
    depth = p.shape[0]
    assert seqlen % FFN_ROW_BLOCK == 0 and seqlen % ATTN_Q_BLOCK == 0 and bsz % SUBLANES == 0
    n = bsz * seqlen
    row_vec = lambda v: v.reshape(1, D_MODEL)
    x = x.reshape(n, D_MODEL)
    s5_weights = jax.vmap(_s5_weights)(ssm_lam_re, ssm_lam_im, ssm_log_dt, ssm_b_re, ssm_b_im,
                                       ssm_c_re, ssm_c_im)
    for i in range(depth):
        kind, slot = i % N_MIXERS, i // N_MIXERS
        gain = row_vec(norm_mix[i])
        pending = {}
        if kind == 0:
            w_glu = ssm_w_glu[slot].astype(BF16)
            x = _s5_mixer(x, bsz, seqlen, gain, row_vec(ssm_d[slot]), s5_weights, slot,
                          w_glu[:, :D_MODEL], w_glu[:, D_MODEL:])
        elif kind == 1:
            w_in = conv_w_in[slot].astype(BF16)
            x = _conv_mixer(x, bsz, seqlen, gain, w_in[:, :D_MODEL],
                            w_in[:, D_MODEL:2 * D_MODEL], w_in[:, 2 * D_MODEL:],
                            conv_w[slot], conv_w_out[slot].astype(BF16))
        else:
            pending = dict(mix=_fox_mixer(x, bsz, seqlen, gain, fox_w_in[slot], fox_b_f[slot]),
                           wo=fox_w_out[slot].astype(BF16))
        x = _ffn_ple(x, p.reshape(depth, n, PLE_DIM), i, row_vec(norm_ffn[i]), row_vec(norm_ple[i]),
                     row_vec(norm_final), mlp_w1[i].astype(BF16), mlp_w2[i].astype(BF16),
                     ple_gate_w[i].astype(BF16), ple_w[i].astype(BF16),
                     final_norm=(i == depth - 1), **pending)
    return x.reshape(bsz, seqlen, D_MODEL)
```

```python
import functools
import math

import jax
import jax.numpy as jnp
from jax import lax
from jax.experimental import pallas as pl
from jax.experimental.pallas import tpu as pltpu

D_MODEL = 1024
D_FF = 4 * D_MODEL
PLE_DIM = 256
NORM_EPS = 1e-6
N_MIXERS = 3
SSM_GROUP = 16
SSM_GROUPS = D_MODEL // SSM_GROUP
SSM_STATE = 64
CONV_WIDTH = 3
FOX_HEAD_DIM = 64
FOX_HEADS = D_MODEL // FOX_HEAD_DIM
FOX_PAIRS = FOX_HEADS // 2
LOG2_E = math.log2(math.e)
FOX_SCORE_SCALE = FOX_HEAD_DIM ** -0.5 * LOG2_E
FOX_BIAS_LANES = 6
LANES = 128
SUBLANES = 8
VMEM_LIMIT = 56 * 1024 * 1024

ROW_BLOCK = 512
FFN_ROW_BLOCK = 1024
FF_CHUNK = 1024
ATTN_Q_BLOCK = 256
ATTN_LOOKAHEAD = 2
ATTN_SLOTS = ATTN_LOOKAHEAD + 2

SSM_STEP = 4
SSM_FOLD = SSM_STEP * LANES
SLABS = D_MODEL // LANES
SLAB_GROUPS = LANES // SSM_GROUP
SLAB_STATE = SLAB_GROUPS * SSM_STATE
SCAN_TIME_BLOCK = 1024
INTERLEAVE_TIME_BLOCK = 128
GLU_TIME_BLOCK = 128

BF16 = jnp.bfloat16
F32 = jnp.float32


def _rms(x, gain):
    return x * lax.rsqrt(jnp.mean(x * x, axis=-1, keepdims=True) + NORM_EPS) * gain


def _dot(a, b):
    return jnp.dot(a, b, preferred_element_type=F32)


def _params(n_axes):
    return pltpu.CompilerParams(dimension_semantics=("arbitrary",) * n_axes,
                                vmem_limit_bytes=VMEM_LIMIT)


def _resident(shape):
    nd = len(shape)
    return pl.BlockSpec(shape, lambda *_: (0,) * nd, pipeline_mode=pl.Buffered(1))


def _ffn_ple_kernel(*refs, final_norm, pre_proj):
    if pre_proj:
        (x_ref, p_ref, gf_ref, gp_ref, gl_ref, w1_ref, w2_ref, wg_ref, wp_ref, mix_ref, wo_ref,
         o_ref, h_ref) = refs
        mix = jnp.concatenate([mix_ref[j] for j in range(mix_ref.shape[0])], axis=1)
        o_ref[...] = x_ref[...] + _dot(mix, wo_ref[...])
        x_in = o_ref
    else:
        (x_ref, p_ref, gf_ref, gp_ref, gl_ref, w1_ref, w2_ref, wg_ref, wp_ref,
         o_ref, h_ref) = refs
        x_in = x_ref
    h_ref[...] = _rms(x_in[...], gf_ref[...]).astype(BF16)

    def mlp_chunk(col):
        a = jnp.maximum(_dot(h_ref[...], w1_ref[:, pl.ds(col, FF_CHUNK)]), 0.0)
        return _dot((a * a).astype(BF16), w2_ref[pl.ds(col, FF_CHUNK), :])

    o_ref[...] = x_in[...] + mlp_chunk(0)

    def chunk(c, carry):
        o_ref[...] += mlp_chunk(pl.multiple_of(c * FF_CHUNK, FF_CHUNK))
        return carry

    lax.fori_loop(1, D_FF // FF_CHUNK, chunk, 0)
    x = o_ref[...]
    gate = jax.nn.sigmoid(_dot(_rms(x, gp_ref[...]).astype(BF16), wg_ref[...]))
    x = x + _dot(p_ref[...].astype(BF16), wp_ref[...]) * gate
    if final_norm:
        x = _rms(x, gl_ref[...])
    o_ref[...] = x


def _ffn_ple(x, p, layer, g_ffn, g_ple, g_last, w1, w2, wg, wp, final_norm, mix=None, wo=None):
    n = x.shape[0]
    row = lambda w: pl.BlockSpec((FFN_ROW_BLOCK, w), lambda i: (i, 0))
    in_specs = [row(D_MODEL),
                pl.BlockSpec((None, FFN_ROW_BLOCK, PLE_DIM), lambda i: (layer, i, 0)),
                _resident((1, D_MODEL)), _resident((1, D_MODEL)), _resident((1, D_MODEL)),
                _resident((D_MODEL, D_FF)), _resident((D_FF, D_MODEL)),
                _resident((D_MODEL, D_MODEL)), _resident((PLE_DIM, D_MODEL))]
    args = [x, p, g_ffn, g_ple, g_last, w1, w2, wg, wp]
    if mix is not None:
        _, slabs, seqlen, _ = mix.shape
        tblocks = seqlen // FFN_ROW_BLOCK
        in_specs += [pl.BlockSpec((None, slabs, FFN_ROW_BLOCK, LANES),
                                  lambda i: (i // tblocks, 0, i % tblocks, 0)),
                     _resident((D_MODEL, D_MODEL))]
        args += [mix, wo]
    return pl.pallas_call(
        functools.partial(_ffn_ple_kernel, final_norm=final_norm, pre_proj=mix is not None),
        grid=(n // FFN_ROW_BLOCK,),
        in_specs=in_specs,
        out_specs=row(D_MODEL),
        out_shape=jax.ShapeDtypeStruct((n, D_MODEL), F32),
        scratch_shapes=[pltpu.VMEM((FFN_ROW_BLOCK, D_MODEL), BF16)],
        compiler_params=_params(1),
        name="ffn_ple",
    )(*args)


def _s5_interleave_kernel(x_ref, g_ref, o_ref):
    for b in range(SUBLANES):
        o_ref[:, b, :] = _rms(x_ref[b], g_ref[...])


def _s5_interleave(x4, gain):
    groups, _, seqlen, _ = x4.shape
    tb = INTERLEAVE_TIME_BLOCK
    return pl.pallas_call(
        _s5_interleave_kernel, grid=(groups, seqlen // tb),
        in_specs=[pl.BlockSpec((None, SUBLANES, tb, D_MODEL), lambda g, t: (g, 0, t, 0)),
                  _resident((1, D_MODEL))],
        out_specs=pl.BlockSpec((None, tb, SUBLANES, D_MODEL), lambda g, t: (g, t, 0, 0)),
        out_shape=jax.ShapeDtypeStruct((groups, seqlen, SUBLANES, D_MODEL), F32),
        compiler_params=_params(2), name="s5_interleave",
    )(x4, gain)


def _s5_scan_kernel(u_ref, bs_ref, cs_ref, ds_ref, a_ref, y_ref, bu_ref, xs_ref, st_ref):
    steps = u_ref.shape[0]
    rows = steps * SUBLANES

    @pl.when(pl.program_id(2) == 0)
    def _():
        st_ref[...] = jnp.zeros_like(st_ref)

    u = jnp.concatenate([u_ref[:, i].reshape(rows, LANES).astype(BF16)
                         for i in range(SSM_STEP)], axis=1)
    bu_ref[...] = _dot(u, bs_ref[...])
    a_re = jnp.broadcast_to(a_ref[0:1, :], (SUBLANES, SLAB_STATE))
    a_im = jnp.broadcast_to(a_ref[1:2, :], (SUBLANES, SLAB_STATE))

    def step(k, carry):
        s_re, s_im = carry
        r0 = pl.multiple_of(k * SUBLANES, SUBLANES)
        xs_ref[pl.ds(r0, SUBLANES), 0:SLAB_STATE] = s_re
        xs_ref[pl.ds(r0, SUBLANES), SLAB_STATE:] = s_im
        b_re = bu_ref[pl.ds(r0, SUBLANES), 0:SLAB_STATE]
        b_im = bu_ref[pl.ds(r0, SUBLANES), SLAB_STATE:]
        return (a_re * s_re - a_im * s_im + b_re, a_re * s_im + a_im * s_re + b_im)

    s_re, s_im = lax.fori_loop(0, steps, step, (st_ref[0], st_ref[1]))
    st_ref[0] = s_re
    st_ref[1] = s_im
    y = _dot(xs_ref[...].astype(BF16), cs_ref[...]) + _dot(u, ds_ref[...])
    for j in range(SSM_STEP):
        y_ref[:, j] = y[:, j * LANES:(j + 1) * LANES].reshape(steps, SUBLANES, LANES)


def _s5_scan(u5, bs, cs, ds, a_pow, slot):
    groups, total_steps = u5.shape[0], u5.shape[1]
    steps = min(SCAN_TIME_BLOCK // SSM_STEP, total_steps)
    rows = steps * SUBLANES
    act = pl.BlockSpec((None, steps, SSM_STEP, SUBLANES, LANES), lambda s, g, t: (g, t, 0, 0, s))
    per_slab = lambda *shape: pl.BlockSpec((None, None) + shape,
                                           lambda s, g, t: (slot, s) + (0,) * len(shape))
    return pl.pallas_call(
        _s5_scan_kernel, grid=(SLABS, groups, total_steps // steps),
        in_specs=[act, per_slab(SSM_FOLD, 2 * SLAB_STATE), per_slab(2 * SLAB_STATE, SSM_FOLD),
                  per_slab(SSM_FOLD, SSM_FOLD), per_slab(2, SLAB_STATE)],
        out_specs=act, out_shape=jax.ShapeDtypeStruct(u5.shape, F32),
        scratch_shapes=[pltpu.VMEM((rows, 2 * SLAB_STATE), F32),
                        pltpu.VMEM((rows, 2 * SLAB_STATE), F32),
                        pltpu.VMEM((2, SUBLANES, SLAB_STATE), F32)],
        compiler_params=_params(3), name="s5_scan",
    )(u5, bs, cs, ds, a_pow)


def _s5_glu_kernel(x_ref, y_ref, g_ref, d_ref, wa_ref, wg_ref, o_ref, ybuf_ref):
    tb = x_ref.shape[1]
    for b in range(SUBLANES):
        ybuf_ref[b * tb:(b + 1) * tb, :] = y_ref[:, b, :]
    x = x_ref[...].reshape(SUBLANES * tb, D_MODEL)
    u = _rms(x, g_ref[...])
    z = jax.nn.gelu(ybuf_ref[...] + d_ref[...] * u).astype(BF16)
    out = x + _dot(z, wa_ref[...]) * jax.nn.sigmoid(_dot(z, wg_ref[...]))
    o_ref[...] = out.reshape(SUBLANES, tb, D_MODEL)


def _s5_glu(x4, y4, gain, d_skip, wa, wg):
    groups, _, seqlen, _ = x4.shape
    tb = GLU_TIME_BLOCK
    nat = pl.BlockSpec((None, SUBLANES, tb, D_MODEL), lambda g, t: (g, 0, t, 0))
    return pl.pallas_call(
        _s5_glu_kernel, grid=(groups, seqlen // tb),
        in_specs=[nat, pl.BlockSpec((None, tb, SUBLANES, D_MODEL), lambda g, t: (g, t, 0, 0)),
                  _resident((1, D_MODEL)), _resident((1, D_MODEL)),
                  _resident((D_MODEL, D_MODEL)), _resident((D_MODEL, D_MODEL))],
        out_specs=nat, out_shape=jax.ShapeDtypeStruct(x4.shape, F32),
        scratch_shapes=[pltpu.VMEM((SUBLANES * tb, D_MODEL), F32)],
        compiler_params=_params(2), name="s5_glu",
    )(x4, y4, gain, d_skip, wa, wg)


def _s5_weights(lam_re, lam_im, log_dt, b_re, b_im, c_re, c_im):
    hp = lax.Precision.HIGHEST
    S, G, H, P = SSM_STEP, SSM_GROUPS, SSM_GROUP, SSM_STATE
    dt = jnp.exp(log_dt)[:, None]

    def powers(n):
        n = n.astype(F32)[:, None, None]
        mag = jnp.exp(n * (lam_re * dt))
        return mag * jnp.cos(n * (lam_im * dt)), mag * jnp.sin(n * (lam_im * dt))

    pw_re, pw_im = powers(jnp.arange(S + 1))
    rev_re, rev_im = powers(S - 1 - jnp.arange(S))
    nr, ni = pw_re[1] - 1.0, pw_im[1]
    den = lam_re * lam_re + lam_im * lam_im
    coef_re = (nr * lam_re + ni * lam_im) / den
    coef_im = (ni * lam_re - nr * lam_im) / den
    bb_re = coef_re[..., None] * b_re - coef_im[..., None] * b_im
    bb_im = coef_re[..., None] * b_im + coef_im[..., None] * b_re
    ca_re = c_re[None] * pw_re[:, :, None, :] - c_im[None] * pw_im[:, :, None, :]
    ca_im = c_re[None] * pw_im[:, :, None, :] + c_im[None] * pw_re[:, :, None, :]
    bs_re = rev_re[:, :, :, None] * bb_re[None] - rev_im[:, :, :, None] * bb_im[None]
    bs_im = rev_re[:, :, :, None] * bb_im[None] + rev_im[:, :, :, None] * bb_re[None]
    kern = (jnp.einsum('tghp,gpk->tghk', ca_re[:S], bb_re, precision=hp)
            - jnp.einsum('tghp,gpk->tghk', ca_im[:S], bb_im, precision=hp))
    lag = jnp.arange(S)[None, :] - jnp.arange(S)[:, None]
    toep = jnp.where((lag >= 0)[:, :, None, None, None], kern[jnp.maximum(lag, 0)], 0.0)

    def block_diag(t):
        na, nb, _, nx, ny = t.shape
        t = t.reshape(na, nb, SLABS, SLAB_GROUPS, nx, ny).transpose(2, 1, 3, 5, 0, 4)
        compact = t.reshape(SLABS, nb * SLAB_GROUPS * ny, na * nx)
        col = jnp.arange(na * SLAB_GROUPS * nx)
        src = (col // (SLAB_GROUPS * nx)) * nx + col % nx
        widen = (jnp.arange(na * nx)[:, None] == src[None, :]).astype(F32)
        row_group = (jnp.arange(nb * SLAB_GROUPS * ny) // ny) % SLAB_GROUPS
        col_group = (col // nx) % SLAB_GROUPS
        wide = jnp.einsum('srk,kc->src', compact, widen, precision=hp)
        return jnp.where(row_group[:, None] == col_group[None, :], wide, 0.0).astype(BF16)

    bs = block_diag(jnp.stack([bs_re, bs_im], axis=0))
    cs = block_diag(jnp.stack([ca_re[1:], -ca_im[1:]], axis=1))
    ds = block_diag(toep.transpose(1, 0, 2, 3, 4))
    a_pow = jnp.stack([pw_re[S].reshape(SLABS, SLAB_STATE),
                       pw_im[S].reshape(SLABS, SLAB_STATE)], axis=1)
    return bs, cs, ds, a_pow


def _s5_mixer(x, bsz, seqlen, gain, d_skip, weights, slot, wa, wg):
    groups = bsz // SUBLANES
    x4 = x.reshape(groups, SUBLANES, seqlen, D_MODEL)
    h = _s5_interleave(x4, gain)
    folded = (groups, seqlen // SSM_STEP, SSM_STEP, SUBLANES, D_MODEL)
    y = _s5_scan(h.reshape(folded), *weights, slot)
    out = _s5_glu(x4, y.reshape(h.shape), gain, d_skip, wa, wg)
    return out.reshape(bsz * seqlen, D_MODEL)


def _conv_kernel(x_ref, g_ref, wb_ref, wc_ref, wv_ref, cw_ref, wo_ref, o_ref, z_ref):
    t = pl.program_id(1)
    rows = x_ref.shape[0]

    @pl.when(t == 0)
    def _():
        z_ref[0:SUBLANES, :] = jnp.zeros((SUBLANES, D_MODEL), F32)

    x = x_ref[...]
    h = _rms(x, g_ref[...]).astype(BF16)
    z_ref[SUBLANES:, :] = _dot(h, wc_ref[...]) * _dot(h, wv_ref[...])
    conv = cw_ref[CONV_WIDTH - 1:CONV_WIDTH, :] * z_ref[SUBLANES:, :]
    for tap in range(CONV_WIDTH - 1):
        back = CONV_WIDTH - 1 - tap
        conv = conv + cw_ref[tap:tap + 1, :] * z_ref[SUBLANES - back:SUBLANES - back + rows, :]
    z_ref[0:SUBLANES, :] = z_ref[rows:rows + SUBLANES, :]
    gated = (_dot(h, wb_ref[...]) * conv).astype(BF16)
    o_ref[...] = x + _dot(gated, wo_ref[...])


def _conv_mixer(x, bsz, seqlen, gain, wb, wc, wv, conv_w, wo):
    tblocks = seqlen // ROW_BLOCK
    row = pl.BlockSpec((ROW_BLOCK, D_MODEL), lambda b, t: (b * tblocks + t, 0))
    sq = _resident((D_MODEL, D_MODEL))
    return pl.pallas_call(
        _conv_kernel, grid=(bsz, tblocks),
        in_specs=[row, _resident((1, D_MODEL)), sq, sq, sq,
                  _resident((CONV_WIDTH, D_MODEL)), sq],
        out_specs=row, out_shape=jax.ShapeDtypeStruct(x.shape, F32),
        scratch_shapes=[pltpu.VMEM((ROW_BLOCK + SUBLANES, D_MODEL), F32)],
        compiler_params=_params(2), name="conv_mixer",
    )(x, gain, wb, wc, wv, conv_w, wo)


def _split3(x):
    hi = x.astype(BF16)
    rest = x - hi.astype(F32)
    mid = rest.astype(BF16)
    return hi, mid, (rest - mid.astype(F32)).astype(BF16)


def _fox_proj_kernel(x_ref, g_ref, wq_ref, wk_ref, wv_ref, wf_ref, bf_ref,
                     q_ref, k_ref, v_ref, qb_ref, kb_ref, carry_ref):
    t = pl.program_id(1)
    rows = x_ref.shape[0]

    @pl.when(t == 0)
    def _():
        carry_ref[...] = jnp.zeros_like(carry_ref)

    h = _rms(x_ref[...], g_ref[...]).astype(BF16)
    for w_ref, out_ref, mult in ((wq_ref, q_ref, FOX_SCORE_SCALE), (wk_ref, k_ref, None),
                                 (wv_ref, v_ref, None)):
        proj = _dot(h, w_ref[...])
        proj = (proj if mult is None else proj * mult).astype(BF16)
        for j in range(FOX_PAIRS):
            out_ref[j] = proj[:, j * LANES:(j + 1) * LANES]
    logit = _dot(h, wf_ref[...]) + bf_ref[...]
    log_f = jnp.minimum(logit, 0.0) - jnp.log1p(jnp.exp(-jnp.abs(logit)))
    tri = (lax.broadcasted_iota(jnp.int32, (rows, rows), 0)
           >= lax.broadcasted_iota(jnp.int32, (rows, rows), 1)).astype(BF16)
    cum = sum(_dot(tri, piece) for piece in _split3(log_f)) + carry_ref[0:1, :]
    carry_ref[0:1, :] = cum[rows - 1:rows, :]
    src = lax.broadcasted_iota(jnp.int32, (LANES, LANES), 0)
    dst = lax.broadcasted_iota(jnp.int32, (LANES, LANES), 1)
    place = lambda j: ((dst == FOX_BIAS_LANES * src + j) & (src < FOX_HEADS)).astype(BF16)
    pieces = _split3(cum * LOG2_E)
    lane = lax.broadcasted_iota(jnp.int32, (1, LANES), 1)
    slot = lane % FOX_BIAS_LANES
    used = lane < FOX_BIAS_LANES * FOX_HEADS
    ones_q = (used & (slot >= 3)).astype(F32)
    ones_k = (used & (slot < 3)).astype(F32)
    qb_ref[...] = (sum(_dot(pieces[j], place(j)) for j in range(3)) + ones_q).astype(BF16)
    kb_ref[...] = (ones_k - sum(_dot(pieces[j], place(3 + j)) for j in range(3))).astype(BF16)


def _fox_proj(x, bsz, seqlen, gain, wq, wk, wv, wf, bf):
    tblocks = seqlen // ROW_BLOCK
    idx = lambda b, t: (b * tblocks + t, 0)
    row = pl.BlockSpec((ROW_BLOCK, D_MODEL), idx)
    bias = pl.BlockSpec((ROW_BLOCK, LANES), idx)
    pairs = pl.BlockSpec((None, FOX_PAIRS, ROW_BLOCK, LANES), lambda b, t: (b, 0, t, 0))
    sq = _resident((D_MODEL, D_MODEL))
    qkv = jax.ShapeDtypeStruct((bsz, FOX_PAIRS, seqlen, LANES), BF16)
    bias_shape = jax.ShapeDtypeStruct((x.shape[0], LANES), BF16)
    return pl.pallas_call(
        _fox_proj_kernel, grid=(bsz, tblocks),
        in_specs=[row, _resident((1, D_MODEL)), sq, sq, sq,
                  _resident((D_MODEL, LANES)), _resident((1, LANES))],
        out_specs=[pairs, pairs, pairs, bias, bias],
        out_shape=[qkv, qkv, qkv, bias_shape, bias_shape],
        scratch_shapes=[pltpu.VMEM((SUBLANES, LANES), F32)],
        compiler_params=_params(2), name="fox_proj",
    )(x, gain, wq, wk, wv, wf, bf)


def _fox_attn_kernel(q_ref, k_ref, v_ref, qb_ref, kb_ref, o_ref, kk_ref, vt_ref, s_ref, p_ref):
    pair = pl.program_id(1)
    seqlen = q_ref.shape[1]
    tq = tk = ATTN_Q_BLOCK
    lane = lax.broadcasted_iota(jnp.int32, (1, LANES), 1)
    low = lane < FOX_HEAD_DIM
    keep = (lax.broadcasted_iota(jnp.int32, (tk, tq), 1)
            >= lax.broadcasted_iota(jnp.int32, (tk, tq), 0))
    contract_last = (((1,), (1,)), ((), ()))
    kk_ref[:, 0:LANES] = k_ref[0]
    kk_ref[:, LANES:] = kb_ref[0]
    vt_ref[...] = v_ref[0].astype(F32).T.astype(BF16)
    units = [(qb, e) for qb in range(seqlen // tq) for e in range(2)]

    def score_unit(u):
        qb, e = units[u]
        q0 = qb * tq
        q = q_ref[0, q0:q0 + tq, :]
        q_bias = qb_ref[0, q0:q0 + tq, :]
        first = FOX_BIAS_LANES * (2 * pair + e)
        mine = low if e == 0 else jnp.logical_not(low)
        own_bias = (lane >= first) & (lane < first + FOX_BIAS_LANES)
        qq = jnp.concatenate([jnp.where(mine, q, jnp.zeros_like(q)),
                              jnp.where(own_bias, q_bias, jnp.zeros_like(q_bias))], axis=1)
        scores = lambda r0, rows: lax.dot_general(
            kk_ref[r0:r0 + rows, :], qq, contract_last, preferred_element_type=F32)
        sc = s_ref.at[u % ATTN_SLOTS]
        if qb > 0:
            sc[0:q0, :] = scores(0, q0)
        sc[q0:q0 + tk, :] = jnp.where(keep, scores(q0, tk), -jnp.inf)

    def softmax_unit(u):
        qb, e = units[u]
        kv = qb * tq + tk
        sc, pr = s_ref.at[u % ATTN_SLOTS], p_ref.at[u % ATTN_SLOTS]
        m = sc[0:SUBLANES, :]
        for r in range(SUBLANES, kv, SUBLANES):
            m = jnp.maximum(m, sc[r:r + SUBLANES, :])
        m = jnp.max(m, axis=0, keepdims=True)
        l = jnp.zeros((SUBLANES, tq), F32)
        for r in range(0, kv, 2 * SUBLANES):
            p = jnp.exp2(sc[r:r + 2 * SUBLANES, :] - m)
            pr[r:r + 2 * SUBLANES, :] = p.astype(BF16)
            l = l + p[0:SUBLANES] + p[SUBLANES:]
        l = jnp.sum(l, axis=0, keepdims=True)
        out_t = _dot(vt_ref[e * FOX_HEAD_DIM:(e + 1) * FOX_HEAD_DIM, 0:kv], pr[0:kv, :])
        return out_t, l

    for u in range(ATTN_LOOKAHEAD):
        score_unit(u)
    pending = []
    for u, (qb, e) in enumerate(units):
        if u + ATTN_LOOKAHEAD < len(units):
            score_unit(u + ATTN_LOOKAHEAD)
        if e == 0 and pending:
            done_qb, parts = pending.pop()
            o_ref[0, done_qb * tq:(done_qb + 1) * tq, :] = jnp.concatenate(
                [out_t / l for out_t, l in parts], axis=0).T.astype(BF16)
        if e == 0:
            pending.append((qb, []))
        pending[-1][1].append(softmax_unit(u))
    done_qb, parts = pending.pop()
    o_ref[0, done_qb * tq:(done_qb + 1) * tq, :] = jnp.concatenate(
        [out_t / l for out_t, l in parts], axis=0).T.astype(BF16)


def _fox_attn(q, k, v, q_bias, k_bias):
    bsz, _, seqlen, _ = q.shape
    seq = pl.BlockSpec((None, 1, seqlen, LANES), lambda b, j: (b, j, 0, 0))
    shared = pl.BlockSpec((1, seqlen, LANES), lambda b, j: (b, 0, 0))
    return pl.pallas_call(
        _fox_attn_kernel, grid=(bsz, FOX_PAIRS),
        in_specs=[seq, seq, seq, shared, shared],
        out_specs=seq, out_shape=jax.ShapeDtypeStruct(q.shape, BF16),
        scratch_shapes=[pltpu.VMEM((seqlen, 2 * LANES), BF16),
                        pltpu.VMEM((LANES, seqlen), BF16),
                        pltpu.VMEM((ATTN_SLOTS, seqlen, ATTN_Q_BLOCK), F32),
                        pltpu.VMEM((ATTN_SLOTS, seqlen, ATTN_Q_BLOCK), BF16)],
        compiler_params=_params(2), name="fox_attn",
    )(q, k, v, q_bias, k_bias)


def _fox_mixer(x, bsz, seqlen, gain, w_in, b_f):
    wq, wk, wv = (w_in[:, i * D_MODEL:(i + 1) * D_MODEL].astype(BF16) for i in range(3))
    wf = jnp.pad(w_in[:, 3 * D_MODEL:], ((0, 0), (0, LANES - FOX_HEADS))).astype(BF16)
    bf = jnp.pad(b_f, (0, LANES - FOX_HEADS)).reshape(1, LANES)
    q, k, v, q_bias, k_bias = _fox_proj(x, bsz, seqlen, gain, wq, wk, wv, wf, bf)
    bias_shape = (bsz, seqlen, LANES)
    return _fox_attn(q, k, v, q_bias.reshape(bias_shape), k_bias.reshape(bias_shape))


def kernel(x, p, norm_mix, norm_ffn, norm_ple, norm_final, ssm_lam_re, ssm_lam_im, ssm_log_dt,
           ssm_b_re, ssm_b_im, ssm_c_re, ssm_c_im, ssm_d, ssm_w_glu, conv_w_in, conv_w,
           conv_w_out, fox_w_in, fox_b_f, fox_w_out, mlp_w1, mlp_w2, ple_w, ple_gate_w):
    bsz, seqlen, _ = x.shape
    depth = p.shape[0]
    assert seqlen % FFN_ROW_BLOCK == 0 and seqlen % ATTN_Q_BLOCK == 0 and bsz % SUBLANES == 0
    n = bsz * seqlen
    row_vec = lambda v: v.reshape(1, D_MODEL)
    x = x.reshape(n, D_MODEL)
    s5_weights = jax.vmap(_s5_weights)(ssm_lam_re, ssm_lam_im, ssm_log_dt, ssm_b_re, ssm_b_im,
                                       ssm_c_re, ssm_c_im)
    for i in range(depth):
        kind, slot = i % N_MIXERS, i // N_MIXERS
        gain = row_vec(norm_mix[i])
        pending = {}
        if kind == 0:
            w_glu = ssm_w_glu[slot].astype(BF16)
            x = _s5_mixer(x, bsz, seqlen, gain, row_vec(ssm_d[slot]), s5_weights, slot,
                          w_glu[:, :D_MODEL], w_glu[:, D_MODEL:])
        elif kind == 1:
            w_in = conv_w_in[slot].astype(BF16)
            x = _conv_mixer(x, bsz, seqlen, gain, w_in[:, :D_MODEL],
                            w_in[:, D_MODEL:2 * D_MODEL], w_in[:, 2 * D_MODEL:],
                            conv_w[slot], conv_w_out[slot].astype(BF16))
        else:
            pending = dict(mix=_fox_mixer(x, bsz, seqlen, gain, fox_w_in[slot], fox_b_f[slot]),
                           wo=fox_w_out[slot].astype(BF16))
        x = _ffn_ple(x, p.reshape(depth, n, PLE_DIM), i, row_vec(norm_ffn[i]), row_vec(norm_ple[i]),
                     row_vec(norm_final), mlp_w1[i].astype(BF16), mlp_w2[i].astype(BF16),
                     ple_gate_w[i].astype(BF16), ple_w[i].astype(BF16),
                     final_norm=(i == depth - 1), **pending)
    return x.reshape(bsz, seqlen, D_MODEL)
```
